```python
import math
import jax, jax.numpy as jnp
from jax import lax
import numpy as np


D_MODEL = 1024
BATCH = 8
SEQ = 8192
DEPTH = 4
DEC_BATCH = 8
DEC_SEQ = 4096
PAST_LEN = 128

ROPE_THETA = 10000.0
NORM_EPS = 1e-5
Q_BLOCK = 128
D_FF = 2816
A_HEADS = 8
A_NOPE = 64
A_ROPE = 32
A_V = 64
A_Q_LORA = 384
A_KV_LORA = 256
B_HEADS = 4
B_D = 32
B_V = 2 * B_D
C_HEADS = 4
C_DIM = 64
C_BRANCHES = ((128, 1), (512, 4), (2048, 16))
IN_SPLITS = (A_Q_LORA, A_KV_LORA, A_ROPE,
             B_HEADS * 2 * B_D, B_HEADS * 2 * B_D, B_HEADS * B_V,
             C_HEADS * C_DIM, C_HEADS * C_DIM, C_HEADS * C_DIM)
IN_COLS = sum(IN_SPLITS)
MIX_WIDTH = A_HEADS * A_V + B_HEADS * B_V + C_HEADS * C_DIM
ALPHA = (2 * DEPTH) ** 0.25
BETA = (8 * DEPTH) ** -0.25

kernel_name = 'hybrid_mla_diff_dilated_encoder'


def layer_norm(x, g, b):
    xf = x.astype(jnp.float32)
    mu = jnp.mean(xf, axis=-1, keepdims=True)
    var = jnp.mean(jnp.square(xf - mu), axis=-1, keepdims=True)
    y = (xf - mu) * lax.rsqrt(var + NORM_EPS) * g.astype(jnp.float32) + b.astype(jnp.float32)
    return y.astype(x.dtype)


def rms_norm(x, g):
    xf = x.astype(jnp.float32)
    y = xf * lax.rsqrt(jnp.mean(jnp.square(xf), axis=-1, keepdims=True) + NORM_EPS) * g.astype(jnp.float32)
    return y.astype(x.dtype)


def apply_rope(x):
    seq, dim = x.shape[1], x.shape[-1]
    inv = 1.0 / (ROPE_THETA ** (jnp.arange(0, dim, 2, dtype=jnp.float32) / dim))
    ang = jnp.arange(seq, dtype=jnp.float32)[:, None] * inv[None, :]
    shp = (seq,) + (1,) * (x.ndim - 3) + (dim // 2,)
    c, s = jnp.cos(ang).reshape(shp), jnp.sin(ang).reshape(shp)
    x1, x2 = jnp.split(x.astype(jnp.float32), 2, axis=-1)
    return jnp.concatenate([x1 * c - x2 * s, x2 * c + x1 * s], axis=-1).astype(x.dtype)


def swiglu(x, wg, wu, wd):
    return (jax.nn.silu(x @ wg) * (x @ wu)) @ wd


def to_query_blocks(q):
    b, s = q.shape[0], q.shape[1]
    nb = s // Q_BLOCK
    qb = q.reshape((b, nb, Q_BLOCK) + q.shape[2:])
    return jnp.moveaxis(qb, 1, 0), nb


def from_query_blocks(o):
    o = jnp.moveaxis(o, 0, 1)
    return o.reshape((o.shape[0], o.shape[1] * o.shape[2]) + o.shape[3:])


def dense_attention(q, k, v, scale):
    qb, _ = to_query_blocks(q)

    def one(qblk):
        s = jnp.einsum('bqhd,bkhd->bhqk', qblk, k).astype(jnp.float32) * scale
        p = jax.nn.softmax(s, axis=-1).astype(v.dtype)
        return jnp.einsum('bhqk,bkhd->bqhd', p, v)

    return from_query_blocks(lax.map(one, qb))


def differential_attention(q, k, v, lam, scale):
    qb, _ = to_query_blocks(q)

    def one(qblk):
        s = jnp.einsum('bqhcd,bkhcd->bchqk', qblk, k).astype(jnp.float32) * scale
        p = jax.nn.softmax(s, axis=-1)
        a = (p[:, 0] - lam * p[:, 1]).astype(v.dtype)
        return jnp.einsum('bhqk,bkhd->bqhd', a, v)

    return from_query_blocks(lax.map(one, qb))


def dilated_branch(q, k, v, window, dilation, scale):
    b, s, h, dh = q.shape
    half = window // (2 * dilation)
    L = s // dilation
    n = b * dilation

    def to_sub(t):
        return t.reshape(b, L, dilation, h, dh).transpose(0, 2, 1, 3, 4).reshape(n, L, h, dh)

    qs, ks, vs = to_sub(q), to_sub(k), to_sub(v)
    nb = -(-L // half)
    lp = nb * half
    qs = jnp.pad(qs, ((0, 0), (0, lp - L), (0, 0), (0, 0))).reshape(n, nb, half, h, dh)

    def neighbours(t):
        tb = jnp.pad(t, ((0, 0), (half, lp - L + half), (0, 0), (0, 0))).reshape(n, nb + 2, half, h, dh)
        return jnp.concatenate([tb[:, :-2], tb[:, 1:-1], tb[:, 2:]], axis=2)

    kb, vb = neighbours(ks), neighbours(vs)
    sc = jnp.einsum('njqhd,njkhd->njhqk', qs, kb).astype(jnp.float32) * scale
    qa = jnp.arange(half)
    kc = jnp.arange(3 * half)
    band = jnp.abs(qa[:, None] + half - kc[None, :]) <= half
    kpos = jnp.arange(nb)[:, None] * half - half + kc[None, :]
    kvalid = (kpos >= 0) & (kpos < L)
    mask = band[None, :, :] & kvalid[:, None, :]
    sc = jnp.where(mask[None, :, None, :, :], sc, -1e30)
    m = jnp.max(sc, axis=-1, keepdims=True)
    p = jnp.exp(sc - m)
    den = jnp.sum(p, axis=-1, keepdims=True)
    o = jnp.einsum('njhqk,njkhd->njqhd', (p / den).astype(v.dtype), vb)
    lse = jnp.swapaxes((m + jnp.log(den))[..., 0], 2, 3)

    def from_sub(t):
        t = t.reshape((n, lp) + t.shape[3:])[:, :L]
        t = t.reshape((b, dilation, L) + t.shape[2:])
        t = jnp.swapaxes(t, 1, 2)
        return t.reshape((b, s) + t.shape[3:])

    return from_sub(o), from_sub(lse)


def token_mixing(h, w_in, q_norm, kv_norm, w_uq, w_ukv, diff_lam, diff_g, w_o, lambda_init):
    b, s, _ = h.shape
    proj = h @ w_in
    offs = list(np.cumsum(IN_SPLITS)[:-1])
    cq, ckv, kpe, qb_, kb_, vb_, qc, kc, vc = jnp.split(proj, offs, axis=-1)

    qa = (rms_norm(cq, q_norm) @ w_uq).reshape(b, s, A_HEADS, A_NOPE + A_ROPE)
    q_nope, q_pe = qa[..., :A_NOPE], apply_rope(qa[..., A_NOPE:])
    kva = (rms_norm(ckv, kv_norm) @ w_ukv).reshape(b, s, A_HEADS, A_NOPE + A_V)
    k_nope, v_a = kva[..., :A_NOPE], kva[..., A_NOPE:]
    k_pe = jnp.broadcast_to(apply_rope(kpe.reshape(b, s, 1, A_ROPE)), (b, s, A_HEADS, A_ROPE))
    o_a = dense_attention(jnp.concatenate([q_nope, q_pe], axis=-1),
                          jnp.concatenate([k_nope, k_pe], axis=-1),
                          v_a, (A_NOPE + A_ROPE) ** -0.5)

    lf = diff_lam.astype(jnp.float32)
    lam = jnp.exp(jnp.sum(lf[0] * lf[1])) - jnp.exp(jnp.sum(lf[2] * lf[3])) + lambda_init
    q2 = apply_rope(qb_.reshape(b, s, B_HEADS, 2, B_D))
    k2 = apply_rope(kb_.reshape(b, s, B_HEADS, 2, B_D))
    o_b = differential_attention(q2, k2, vb_.reshape(b, s, B_HEADS, B_V), lam, B_D ** -0.5)
    o_b = rms_norm(o_b, diff_g) * (1.0 - lambda_init)

    qc = apply_rope(qc.reshape(b, s, C_HEADS, C_DIM))
    kc = apply_rope(kc.reshape(b, s, C_HEADS, C_DIM))
    vc = vc.reshape(b, s, C_HEADS, C_DIM)
    outs, lses = [], []
    for window, dilation in C_BRANCHES:
        o_i, l_i = dilated_branch(qc, kc, vc, window, dilation, C_DIM ** -0.5)
        outs.append(o_i)
        lses.append(l_i)
    wts = jax.nn.softmax(jnp.stack(lses, axis=0), axis=0)
    o_c = jnp.einsum('nbsh,nbshd->bshd', wts.astype(vc.dtype), jnp.stack(outs, axis=0))

    o = jnp.concatenate([o_a.reshape(b, s, -1), o_b.reshape(b, s, -1), o_c.reshape(b, s, -1)], axis=-1)
    return o @ w_o


def setup_inputs(seed: int = 0) -> dict:
    key = jax.random.key(seed)
    ks = jax.random.split(key, 16)
    f = jnp.float32
    nrm = lambda k, shp: jax.random.normal(k, shp, f)
    return {
        'x_prompt': nrm(ks[0], (BATCH, SEQ, D_MODEL)),
        'x_sample': nrm(ks[1], (DEC_BATCH, DEC_SEQ, D_MODEL)),
        'ln_g': 1.0 + 0.01 * nrm(ks[2], (DEPTH, 3, D_MODEL)),
        'ln_b': 0.01 * nrm(ks[3], (DEPTH, 3, D_MODEL)),
        'ffn_w_gate': nrm(ks[4], (DEPTH, 2, D_MODEL, D_FF)) * D_MODEL ** -0.5,
        'ffn_w_up': nrm(ks[5], (DEPTH, 2, D_MODEL, D_FF)) * D_MODEL ** -0.5,
        'ffn_w_down': nrm(ks[6], (DEPTH, 2, D_FF, D_MODEL)) * (D_FF ** -0.5 * BETA),
        'w_in': nrm(ks[7], (DEPTH, D_MODEL, IN_COLS)) * D_MODEL ** -0.5,
        'mla_q_norm': 1.0 + 0.01 * nrm(ks[8], (DEPTH, A_Q_LORA)),
        'mla_kv_norm': 1.0 + 0.01 * nrm(ks[9], (DEPTH, A_KV_LORA)),
        'mla_w_uq': nrm(ks[10], (DEPTH, A_Q_LORA, A_HEADS * (A_NOPE + A_ROPE))) * A_Q_LORA ** -0.5,
        'mla_w_ukv': nrm(ks[11], (DEPTH, A_KV_LORA, A_HEADS * (A_NOPE + A_V))) * A_KV_LORA ** -0.5,
        'diff_lambda': 0.1 * nrm(ks[12], (DEPTH, 4, B_D)),
        'diff_subln': 1.0 + 0.01 * nrm(ks[13], (DEPTH, B_V)),
        'w_out': nrm(ks[14], (DEPTH, MIX_WIDTH, D_MODEL)) * (MIX_WIDTH ** -0.5 * BETA),
    }


def reference(x_prompt, x_sample, ln_g, ln_b, ffn_w_gate, ffn_w_up, ffn_w_down, w_in,
              mla_q_norm, mla_kv_norm, mla_w_uq, mla_w_ukv, diff_lambda, diff_subln, w_out):
    def run(x):
        for i in range(DEPTH):
            lambda_init = 0.8 - 0.6 * math.exp(-0.3 * i)
            x = layer_norm(ALPHA * x + 0.5 * swiglu(x, ffn_w_gate[i, 0], ffn_w_up[i, 0], ffn_w_down[i, 0]),
                           ln_g[i, 0], ln_b[i, 0])
            x = layer_norm(ALPHA * x + token_mixing(x, w_in[i], mla_q_norm[i], mla_kv_norm[i], mla_w_uq[i],
                                                    mla_w_ukv[i], diff_lambda[i], diff_subln[i], w_out[i],
                                                    lambda_init),
                           ln_g[i, 1], ln_b[i, 1])
            x = layer_norm(ALPHA * x + 0.5 * swiglu(x, ffn_w_gate[i, 1], ffn_w_up[i, 1], ffn_w_down[i, 1]),
                           ln_g[i, 2], ln_b[i, 2])
        return x

    y_prompt = run(x_prompt)
    y_sample = run(x_sample)
    return (y_prompt, y_sample)
```

```python
import functools
import math

import jax
import jax.numpy as jnp
from jax import lax
from jax.experimental import pallas as pl
from jax.experimental.pallas import tpu as pltpu

F32 = jnp.float32
BF16 = jnp.bfloat16

D_MODEL = 1024
D_FF = 2816
DEPTH = 4
ROPE_THETA = 10000.0
NORM_EPS = 1e-5
A_HEADS, A_NOPE, A_ROPE, A_V = 8, 64, 32, 64
A_Q_LORA, A_KV_LORA = 384, 256
A_QK_PAD = 128
B_HEADS, B_D = 4, 32
B_V = 2 * B_D
C_HEADS, C_DIM = 4, 64
C_BRANCHES = ((128, 1), (512, 4), (2048, 16))
C_HALF = 64
IN_SPLITS = (A_Q_LORA, A_KV_LORA, A_ROPE, 256, 256, 256, 256, 256, 256)
IN_OFFS = tuple(int(sum(IN_SPLITS[:i])) for i in range(len(IN_SPLITS) + 1))
ALPHA = (2 * DEPTH) ** 0.25
LOG2E = math.log2(math.e)

VMEM_LIMIT_BYTES = 56 * 1024 * 1024
SEQ_TILE = 512
FF_CHUNKS = 2
ATTN_TQ = 512
DIL_QBLK = 128
DIL_KBLK = 256
DIL_SUPER = 2048
NEG_BIG = -1e30


def _cparams(sem):
    return pltpu.CompilerParams(dimension_semantics=sem, vmem_limit_bytes=VMEM_LIMIT_BYTES)


def _layer_norm_rows(y, g, b):
    mu = jnp.mean(y, axis=0, keepdims=True)
    d = y - mu
    var = jnp.mean(d * d, axis=0, keepdims=True)
    return d * lax.rsqrt(var + NORM_EPS) * g + b


def _rms_norm_rows(y, g):
    ms = jnp.mean(y * y, axis=0, keepdims=True)
    return y * lax.rsqrt(ms + NORM_EPS) * g


def _rope_rows(x, c, s, groups, dim):
    half = dim // 2
    t = x.shape[-1]
    xg = x.reshape(groups, 2, half, t)
    x1, x2 = xg[:, 0], xg[:, 1]
    o1 = x1 * c - x2 * s
    o2 = x2 * c + x1 * s
    return jnp.stack([o1, o2], axis=1).reshape(groups * dim, t)


def _ffn_kernel(x_ref, wgu_ref, wd_ref, g_ref, b_ref, o_ref, xb_ref, acc_ref, *, tf):
    j = pl.program_id(2)

    @pl.when(j == 0)
    def _():
        xb_ref[...] = x_ref[0].astype(BF16)

    gu = jnp.dot(wgu_ref[0], xb_ref[...], preferred_element_type=F32)
    gate, up = gu[:tf], gu[tf:]
    h = (gate * jax.nn.sigmoid(gate) * up).astype(BF16)
    part = jnp.dot(wd_ref[0], h, preferred_element_type=F32)

    @pl.when(j == 0)
    def _():
        acc_ref[...] = part

    @pl.when(j == FF_CHUNKS - 1)
    def _():
        y = ALPHA * x_ref[0] + 0.5 * (acc_ref[...] + part)
        o_ref[0] = _layer_norm_rows(y, g_ref[...], b_ref[...])


def _ffn_layer(xT, wguT, wdT, g, b):
    bsz, _, seq = xT.shape
    ts = SEQ_TILE
    tf = D_FF // FF_CHUNKS
    return pl.pallas_call(
        functools.partial(_ffn_kernel, tf=tf),
        grid=(bsz, seq // ts, FF_CHUNKS),
        in_specs=[
            pl.BlockSpec((1, D_MODEL, ts), lambda bi, i, j: (bi, 0, i)),
            pl.BlockSpec((1, 2 * tf, D_MODEL), lambda bi, i, j: (j, 0, 0)),
            pl.BlockSpec((1, D_MODEL, tf), lambda bi, i, j: (j, 0, 0)),
            pl.BlockSpec((D_MODEL, 1), lambda bi, i, j: (0, 0)),
            pl.BlockSpec((D_MODEL, 1), lambda bi, i, j: (0, 0)),
        ],
        out_specs=pl.BlockSpec((1, D_MODEL, ts), lambda bi, i, j: (bi, 0, i)),
        out_shape=jax.ShapeDtypeStruct(xT.shape, F32),
        scratch_shapes=[pltpu.VMEM((D_MODEL, ts), BF16), pltpu.VMEM((D_MODEL, ts), F32)],
        compiler_params=_cparams(("parallel", "parallel", "arbitrary")),
        name="ffn",
    )(xT, wguT, wdT, g, b)


def _inproj_kernel(x_ref, win_ref, gq_ref, gkv_ref, wuq_ref, wukv_ref,
                   c32_ref, s32_ref, c64_ref, s64_ref,
                   qa_ref, ka_ref, va_ref, qd_ref, kd_ref, vd_ref, qc_ref, kc_ref, vc_ref):
    xb = x_ref[0].astype(BF16)
    ts = xb.shape[-1]
    c32, s32 = c32_ref[...], s32_ref[...]
    c64, s64 = c64_ref[...], s64_ref[...]

    def proj(lo, hi):
        return jnp.dot(win_ref[lo:hi, :], xb, preferred_element_type=F32)

    nq = _rms_norm_rows(proj(IN_OFFS[0], IN_OFFS[1]), gq_ref[...]).astype(BF16)
    qa = jnp.dot(wuq_ref[...], nq, preferred_element_type=F32)
    nkv = _rms_norm_rows(proj(IN_OFFS[1], IN_OFFS[2]), gkv_ref[...]).astype(BF16)
    kva = jnp.dot(wukv_ref[...], nkv, preferred_element_type=F32)
    kpe = _rope_rows(proj(IN_OFFS[2], IN_OFFS[3]), c32, s32, 1, A_ROPE)
    qscale = (A_NOPE + A_ROPE) ** -0.5 * LOG2E
    zpad = jnp.zeros((A_QK_PAD - A_NOPE - A_ROPE, ts), F32)
    dqk = A_NOPE + A_ROPE
    for h in range(A_HEADS):
        qh = qa[h * dqk:(h + 1) * dqk]
        q_pe = _rope_rows(qh[A_NOPE:], c32, s32, 1, A_ROPE)
        qa_ref[0, h] = (jnp.concatenate([qh[:A_NOPE], q_pe, zpad], axis=0) * qscale).astype(BF16)
        kvh = kva[h * (A_NOPE + A_V):(h + 1) * (A_NOPE + A_V)]
        kT = jnp.concatenate([kvh[:A_NOPE], kpe, zpad], axis=0)
        ka_ref[0, h] = kT.T.astype(BF16)
        va_ref[0, h, 0] = kvh[A_NOPE:].astype(BF16)

    pb = proj(IN_OFFS[3], IN_OFFS[6])
    qd = _rope_rows(pb[0:256], c32, s32, 2 * B_HEADS, B_D) * (B_D ** -0.5 * LOG2E)
    qd_ref[0] = qd.astype(BF16)
    kd = _rope_rows(pb[256:512], c32, s32, 2 * B_HEADS, B_D)
    kd_ref[0] = kd.T.astype(BF16)
    for h in range(B_HEADS):
        vd_ref[0, h, 0] = pb[512 + h * B_V:512 + (h + 1) * B_V].astype(BF16)

    pc = proj(IN_OFFS[6], IN_OFFS[9])
    qc = _rope_rows(pc[0:256], c64, s64, C_HEADS, C_DIM) * (C_DIM ** -0.5 * LOG2E)
    kc = _rope_rows(pc[256:512], c64, s64, C_HEADS, C_DIM)
    qc_ref[0] = qc.T
    kc_ref[0] = kc.T
    vc_ref[0] = pc[512:768].T


def _inproj_layer(xT, winT, gq, gkv, wuqT, wukvT, rope):
    bsz, _, seq = xT.shape
    ts = SEQ_TILE
    nt = seq // ts
    c32, s32, c64, s64 = rope
    full = lambda shape: pl.BlockSpec(shape, lambda bi, i: (0,) * len(shape))
    tab = lambda rows: pl.BlockSpec((rows, ts), lambda bi, i: (0, i))
    out_shape = (
        jax.ShapeDtypeStruct((bsz, A_HEADS, A_QK_PAD, seq), BF16),
        jax.ShapeDtypeStruct((bsz, A_HEADS, seq, A_QK_PAD), BF16),
        jax.ShapeDtypeStruct((bsz, A_HEADS, nt, A_V, ts), BF16),
        jax.ShapeDtypeStruct((bsz, 2 * B_HEADS * B_D, seq), BF16),
        jax.ShapeDtypeStruct((bsz, seq, 2 * B_HEADS * B_D), BF16),
        jax.ShapeDtypeStruct((bsz, B_HEADS, nt, B_V, ts), BF16),
        jax.ShapeDtypeStruct((bsz, seq, C_HEADS * C_DIM), F32),
        jax.ShapeDtypeStruct((bsz, seq, C_HEADS * C_DIM), F32),
        jax.ShapeDtypeStruct((bsz, seq, C_HEADS * C_DIM), F32),
    )
    out_specs = (
        pl.BlockSpec((1, A_HEADS, A_QK_PAD, ts), lambda bi, i: (bi, 0, 0, i)),
        pl.BlockSpec((1, A_HEADS, ts, A_QK_PAD), lambda bi, i: (bi, 0, i, 0)),
        pl.BlockSpec((1, A_HEADS, 1, A_V, ts), lambda bi, i: (bi, 0, i, 0, 0)),
        pl.BlockSpec((1, 256, ts), lambda bi, i: (bi, 0, i)),
        pl.BlockSpec((1, ts, 256), lambda bi, i: (bi, i, 0)),
        pl.BlockSpec((1, B_HEADS, 1, B_V, ts), lambda bi, i: (bi, 0, i, 0, 0)),
        pl.BlockSpec((1, ts, 256), lambda bi, i: (bi, i, 0)),
        pl.BlockSpec((1, ts, 256), lambda bi, i: (bi, i, 0)),
        pl.BlockSpec((1, ts, 256), lambda bi, i: (bi, i, 0)),
    )
    return pl.pallas_call(
        _inproj_kernel,
        grid=(bsz, nt),
        in_specs=[
            pl.BlockSpec((1, D_MODEL, ts), lambda bi, i: (bi, 0, i)),
            full(winT.shape), full(gq.shape), full(gkv.shape), full(wuqT.shape), full(wukvT.shape),
            tab(A_ROPE // 2), tab(A_ROPE // 2), tab(C_DIM // 2), tab(C_DIM // 2),
        ],
        out_specs=out_specs,
        out_shape=out_shape,
        compiler_params=_cparams(("parallel", "parallel")),
        name="inproj",
    )(xT, winT, gq, gkv, wuqT, wukvT, c32, s32, c64, s64)


def _attn_kernel(q_ref, k_ref, v_ref, o_ref, *, tk, nk, group_rows):
    q = q_ref[...].reshape(q_ref.shape[-2:])
    if group_rows:
        rows = lax.broadcasted_iota(jnp.int32, q.shape, 0)
        lo = pl.program_id(1) * group_rows
        q = jnp.where((rows >= lo) & (rows < lo + group_rows), q, jnp.zeros_like(q))
    tq = q.shape[-1]
    kv = k_ref.at[0] if len(k_ref.shape) == 3 else k_ref.at[0, 0]

    def body(i, carry):
        m, l, acc = carry
        start = pl.multiple_of(i * tk, tk)
        s = jnp.dot(kv[pl.ds(start, tk), :], q, preferred_element_type=F32)
        m_new = jnp.maximum(m, jnp.max(s, axis=0, keepdims=True))
        p = jnp.exp2(s - m_new)
        alpha = jnp.exp2(m - m_new)
        l = alpha * l + jnp.sum(p, axis=0, keepdims=True)
        pv = jnp.dot(v_ref[0, 0, i], p.astype(BF16), preferred_element_type=F32)
        return m_new, l, alpha * acc + pv

    dv = v_ref.shape[-2]
    init = (jnp.full((1, tq), NEG_BIG, F32), jnp.zeros((1, tq), F32), jnp.zeros((dv, tq), F32))
    m, l, acc = lax.fori_loop(0, nk, body, init)
    o_ref[0, 0] = acc / l


def _attention(qT, k, vT, *, heads, shared_qk, v_heads):
    bsz, seq = qT.shape[0], qT.shape[-1]
    nk, dv, tk = vT.shape[2], vT.shape[3], vT.shape[4]
    tq = ATTN_TQ
    vrep = heads // v_heads
    if shared_qk:
        dq_all = qT.shape[1]
        q_spec = pl.BlockSpec((1, dq_all, tq), lambda bi, h, i: (bi, 0, i))
        k_spec = pl.BlockSpec((1, seq, dq_all), lambda bi, h, i: (bi, 0, 0))
        group_rows = dq_all // heads
    else:
        dq = qT.shape[2]
        q_spec = pl.BlockSpec((1, 1, dq, tq), lambda bi, h, i: (bi, h, 0, i))
        k_spec = pl.BlockSpec((1, 1, seq, dq), lambda bi, h, i: (bi, h, 0, 0))
        group_rows = 0
    return pl.pallas_call(
        functools.partial(_attn_kernel, tk=tk, nk=nk, group_rows=group_rows),
        grid=(bsz, heads, seq // tq),
        in_specs=[
            q_spec, k_spec,
            pl.BlockSpec((1, 1, nk, dv, tk), lambda bi, h, i: (bi, h // vrep, 0, 0, 0)),
        ],
        out_specs=pl.BlockSpec((1, 1, dv, tq), lambda bi, h, i: (bi, h, 0, i)),
        out_shape=jax.ShapeDtypeStruct((bsz, heads, dv, seq), F32),
        compiler_params=_cparams(("parallel", "parallel", "arbitrary")),
        name="attn_shared" if shared_qk else "attn",
    )(qT, k, vT)


def _dilated_kernel(q_ref, k_ref, v_ref, o_ref, m_ref, l_ref, acc_ref, *, seq, sup):
    sb = pl.program_id(2)
    p0 = sb * sup
    m_ref[...] = jnp.full(m_ref.shape, NEG_BIG, F32)
    l_ref[...] = jnp.zeros(l_ref.shape, F32)
    acc_ref[...] = jnp.zeros(acc_ref.shape, F32)
    lane = lax.broadcasted_iota(jnp.int32, (1, 2 * C_DIM), 1)
    head_lanes = (lane < C_DIM, lane >= C_DIM)
    qi = lax.broadcasted_iota(jnp.int32, (DIL_QBLK, 1), 0)
    ki = lax.broadcasted_iota(jnp.int32, (1, DIL_KBLK), 1)

    for _, dil in C_BRANCHES:
        cls_len = seq // dil
        blocks = sup // (DIL_QBLK * dil)

        def body(t, carry, dil=dil, cls_len=cls_len, blocks=blocks):
            r = lax.div(t, blocks)
            jb = lax.rem(t, blocks)
            c0 = sb * (sup // dil) + DIL_QBLK * jb
            ks = jnp.clip(c0 - C_HALF, 0, cls_len - DIL_KBLK)
            qrow = r + dil * DIL_QBLK * jb
            qsl = pl.ds(p0 + qrow, DIL_QBLK, stride=dil)
            ssl = pl.ds(qrow, DIL_QBLK, stride=dil)
            ksl = pl.ds(r + dil * ks, DIL_KBLK, stride=dil)
            qv = q_ref[0, qsl, :]
            kb = k_ref[0, ksl, :].astype(BF16)
            vb = v_ref[0, ksl, :].astype(BF16)
            band = jnp.abs((c0 + qi) - (ks + ki)) <= C_HALF
            acc_old = acc_ref[ssl, :]
            acc_new = []
            for hh in range(2):
                qm = jnp.where(head_lanes[hh], qv, 0.0).astype(BF16)
                s = lax.dot_general(qm, kb, (((1,), (1,)), ((), ())), preferred_element_type=F32)
                s = jnp.where(band, s, NEG_BIG)
                m_old = m_ref[hh, ssl, :]
                l_old = l_ref[hh, ssl, :]
                m_new = jnp.maximum(m_old, jnp.max(s, axis=1, keepdims=True))
                p = jnp.exp2(s - m_new)
                alpha = jnp.exp2(m_old - m_new)
                m_ref[hh, ssl, :] = m_new
                l_ref[hh, ssl, :] = alpha * l_old + jnp.sum(p, axis=1, keepdims=True)
                pv = jnp.dot(p.astype(BF16), vb, preferred_element_type=F32)
                acc_new.append(alpha * acc_old + pv)
            acc_ref[ssl, :] = jnp.where(head_lanes[0], acc_new[0], acc_new[1])
            return carry

        lax.fori_loop(0, dil * blocks, body, 0)

    denom = jnp.where(head_lanes[0], l_ref[0], l_ref[1])
    o_ref[0] = acc_ref[...] / denom


def _dilated_attention(qc, kc, vc):
    bsz, seq, width = qc.shape
    sup = min(DIL_SUPER, seq)
    pair = 2 * C_DIM
    in_spec = pl.BlockSpec((1, seq, pair), lambda bi, hp, sb: (bi, 0, hp))
    return pl.pallas_call(
        functools.partial(_dilated_kernel, seq=seq, sup=sup),
        grid=(bsz, width // pair, seq // sup),
        in_specs=[in_spec, in_spec, in_spec],
        out_specs=pl.BlockSpec((1, sup, pair), lambda bi, hp, sb: (bi, sb, hp)),
        out_shape=jax.ShapeDtypeStruct((bsz, seq, width), F32),
        scratch_shapes=[pltpu.VMEM((2, sup, 1), F32), pltpu.VMEM((2, sup, 1), F32),
                        pltpu.VMEM((sup, pair), F32)],
        compiler_params=_cparams(("parallel", "parallel", "arbitrary")),
        name="dilated",
    )(qc, kc, vc)


def _outproj_kernel(x_ref, oa_ref, od_ref, oc_ref, woa_ref, wob_ref, woc_ref, lam_ref, dg_ref,
                    g_ref, b_ref, o_ref, *, lambda_init):
    ts = x_ref.shape[-1]
    lf = lam_ref[...]
    lam = (jnp.exp(jnp.sum(lf[0:1] * lf[1:2], axis=1, keepdims=True))
           - jnp.exp(jnp.sum(lf[2:3] * lf[3:4], axis=1, keepdims=True)) + lambda_init)
    od = od_ref[0].reshape(B_HEADS, 2, B_V, ts)
    ob = od[:, 0] - lam * od[:, 1]
    ms = jnp.mean(ob * ob, axis=1, keepdims=True)
    ob = ob * lax.rsqrt(ms + NORM_EPS) * dg_ref[...] * (1.0 - lambda_init)
    ob = ob.reshape(B_HEADS * B_V, ts).astype(BF16)
    y = jnp.dot(woa_ref[...], oa_ref[0].astype(BF16), preferred_element_type=F32)
    y += jnp.dot(wob_ref[...], ob, preferred_element_type=F32)
    y += lax.dot_general(woc_ref[...], oc_ref[0].astype(BF16), (((1,), (1,)), ((), ())),
                         preferred_element_type=F32)
    o_ref[0] = _layer_norm_rows(ALPHA * x_ref[0] + y, g_ref[...], b_ref[...])


def _outproj_layer(xT, oaT, odT, oc, woaT, wobT, wocT, lam, dg, g, b, lambda_init):
    bsz, _, seq = xT.shape
    ts = SEQ_TILE
    full = lambda a: pl.BlockSpec(a.shape, lambda bi, i: (0,) * a.ndim)
    col = lambda rows: pl.BlockSpec((1, rows, ts), lambda bi, i: (bi, 0, i))
    return pl.pallas_call(
        functools.partial(_outproj_kernel, lambda_init=lambda_init),
        grid=(bsz, seq // ts),
        in_specs=[
            col(D_MODEL), col(A_HEADS * A_V), col(2 * B_HEADS * B_V),
            pl.BlockSpec((1, ts, C_HEADS * C_DIM), lambda bi, i: (bi, i, 0)),
            full(woaT), full(wobT), full(wocT), full(lam), full(dg), full(g), full(b),
        ],
        out_specs=col(D_MODEL),
        out_shape=jax.ShapeDtypeStruct(xT.shape, F32),
        compiler_params=_cparams(("parallel", "parallel")),
        name="outproj",
    )(xT, oaT, odT, oc, woaT, wobT, wocT, lam, dg, g, b)


def _rope_tables(seq, dim):
    inv = 1.0 / (ROPE_THETA ** (jnp.arange(0, dim, 2, dtype=F32) / dim))
    ang = inv[:, None] * jnp.arange(seq, dtype=F32)[None, :]
    return jnp.cos(ang), jnp.sin(ang)


def _prep_layer(i, ln_g, ln_b, ffn_w_gate, ffn_w_up, ffn_w_down, w_in, mla_q_norm, mla_kv_norm,
                mla_w_uq, mla_w_ukv, diff_lambda, diff_subln, w_out):
    tf = D_FF // FF_CHUNKS
    col = lambda v: v.astype(F32)[:, None]
    ffn = []
    for s in range(2):
        wg = ffn_w_gate[i, s].T.reshape(FF_CHUNKS, tf, D_MODEL)
        wu = ffn_w_up[i, s].T.reshape(FF_CHUNKS, tf, D_MODEL)
        wgu = jnp.concatenate([wg, wu], axis=1).astype(BF16)
        wd = ffn_w_down[i, s].T.reshape(D_MODEL, FF_CHUNKS, tf).transpose(1, 0, 2).astype(BF16)
        ffn.append((wgu, wd))
    woT = w_out[i].T.astype(BF16)
    na, nb = A_HEADS * A_V, B_HEADS * B_V
    return dict(
        ffn=ffn,
        ln=[(col(ln_g[i, s]), col(ln_b[i, s])) for s in range(3)],
        winT=w_in[i].T.astype(BF16),
        gq=col(mla_q_norm[i]), gkv=col(mla_kv_norm[i]),
        wuqT=mla_w_uq[i].T.astype(BF16), wukvT=mla_w_ukv[i].T.astype(BF16),
        woaT=woT[:, :na], wobT=woT[:, na:na + nb], wocT=woT[:, na + nb:],
        lam=diff_lambda[i].astype(F32), dg=col(diff_subln[i]),
    )


def _run(x, layers):
    bsz, seq, _ = x.shape
    assert seq % DIL_SUPER == 0 and seq // C_BRANCHES[-1][1] >= DIL_KBLK, seq
    assert seq % SEQ_TILE == 0 and seq % ATTN_TQ == 0 and FF_CHUNKS == 2
    rope = _rope_tables(seq, A_ROPE) + _rope_tables(seq, C_DIM)
    xT = jnp.swapaxes(x, 1, 2)
    for i, w in enumerate(layers):
        lambda_init = 0.8 - 0.6 * math.exp(-0.3 * i)
        xT = _ffn_layer(xT, *w["ffn"][0], *w["ln"][0])
        qaT, ka, vaT, qdT, kd, vdT, qc, kc, vc = _inproj_layer(
            xT, w["winT"], w["gq"], w["gkv"], w["wuqT"], w["wukvT"], rope)
        oaT = _attention(qaT, ka, vaT, heads=A_HEADS, shared_qk=False, v_heads=A_HEADS)
        odT = _attention(qdT, kd, vdT, heads=2 * B_HEADS, shared_qk=True, v_heads=B_HEADS)
        oc = _dilated_attention(qc, kc, vc)
        xT = _outproj_layer(xT, oaT.reshape(bsz, A_HEADS * A_V, seq),
                            odT.reshape(bsz, 2 * B_HEADS * B_V, seq), oc,
                            w["woaT"], w["wobT"], w["wocT"], w["lam"], w["dg"], *w["ln"][1],
                            lambda_init)
        xT = _ffn_layer(xT, *w["ffn"][1], *w["ln"][2])
    return jnp.swapaxes(xT, 1, 2)


def kernel(x_prompt, x_sample, ln_g, ln_b, ffn_w_gate, ffn_w_up, ffn_w_down, w_in, mla_q_norm,
           mla_kv_norm, mla_w_uq, mla_w_ukv, diff_lambda, diff_subln, w_out):
    params = (ln_g, ln_b, ffn_w_gate, ffn_w_up, ffn_w_down, w_in, mla_q_norm, mla_kv_norm,
              mla_w_uq, mla_w_ukv, diff_lambda, diff_subln, w_out)
    layers = [_prep_layer(i, *params) for i in range(DEPTH)]
    return _run(x_prompt, layers), _run(x_sample, layers)
```

```python
import functools
import math

import jax
import jax.numpy as jnp
from jax import lax
from jax.experimental import pallas as pl
from jax.experimental.pallas import tpu as pltpu

F32 = jnp.float32
BF16 = jnp.bfloat16

D_MODEL = 1024
D_FF = 2816
DEPTH = 4
ROPE_THETA = 10000.0
NORM_EPS = 1e-5
A_HEADS, A_NOPE, A_ROPE, A_V = 8, 64, 32, 64
A_Q_LORA, A_KV_LORA = 384, 256
A_QK_PAD = 128
B_HEADS, B_D = 4, 32
B_V = 2 * B_D
C_HEADS, C_DIM = 4, 64
C_BRANCHES = ((128, 1), (512, 4), (2048, 16))
C_HALF = 64
IN_SPLITS = (A_Q_LORA, A_KV_LORA, A_ROPE, 256, 256, 256, 256, 256, 256)
IN_OFFS = tuple(int(sum(IN_SPLITS[:i])) for i in range(len(IN_SPLITS) + 1))
ALPHA = (2 * DEPTH) ** 0.25
LOG2E = math.log2(math.e)

VMEM_LIMIT_BYTES = 56 * 1024 * 1024
SEQ_TILE = 512
FF_CHUNKS = 2
ATTN_TQ = 1024
ATTN_CHAINS = 4
SUM_ROWS = 16
DIL_QBLK = 128
DIL_KBLK = 256
DIL_SUPER = 2048
DIL_UNROLL = 4
NEG_BIG = -1e30


def _cparams(sem):
    return pltpu.CompilerParams(dimension_semantics=sem, vmem_limit_bytes=VMEM_LIMIT_BYTES)


def _layer_norm_rows(y, g, b):
    mu = jnp.mean(y, axis=0, keepdims=True)
    d = y - mu
    var = jnp.mean(d * d, axis=0, keepdims=True)
    return d * lax.rsqrt(var + NORM_EPS) * g + b


def _rms_norm_rows(y, g):
    ms = jnp.mean(y * y, axis=0, keepdims=True)
    return y * lax.rsqrt(ms + NORM_EPS) * g


def _rope_rows(x, c, s, groups, dim):
    half = dim // 2
    t = x.shape[-1]
    xg = x.reshape(groups, 2, half, t)
    x1, x2 = xg[:, 0], xg[:, 1]
    o1 = x1 * c - x2 * s
    o2 = x2 * c + x1 * s
    return jnp.stack([o1, o2], axis=1).reshape(groups * dim, t)


def _ffn_kernel(x_ref, wgu_ref, wd_ref, g_ref, b_ref, o_ref, xb_ref, acc_ref, *, tf):
    j = pl.program_id(2)

    @pl.when(j == 0)
    def _():
        xb_ref[...] = x_ref[0].astype(BF16)

    gu = jnp.dot(wgu_ref[0], xb_ref[...], preferred_element_type=F32)
    gate, up = gu[:tf], gu[tf:]
    h = (gate * jax.nn.sigmoid(gate) * up).astype(BF16)
    part = jnp.dot(wd_ref[0], h, preferred_element_type=F32)

    @pl.when(j == 0)
    def _():
        acc_ref[...] = part

    @pl.when(j == FF_CHUNKS - 1)
    def _():
        y = ALPHA * x_ref[0] + 0.5 * (acc_ref[...] + part)
        o_ref[0] = _layer_norm_rows(y, g_ref[...], b_ref[...])


def _ffn_layer(xT, wguT, wdT, g, b):
    bsz, _, seq = xT.shape
    ts = SEQ_TILE
    tf = D_FF // FF_CHUNKS
    return pl.pallas_call(
        functools.partial(_ffn_kernel, tf=tf),
        grid=(bsz, seq // ts, FF_CHUNKS),
        in_specs=[
            pl.BlockSpec((1, D_MODEL, ts), lambda bi, i, j: (bi, 0, i)),
            pl.BlockSpec((1, 2 * tf, D_MODEL), lambda bi, i, j: (j, 0, 0)),
            pl.BlockSpec((1, D_MODEL, tf), lambda bi, i, j: (j, 0, 0)),
            pl.BlockSpec((D_MODEL, 1), lambda bi, i, j: (0, 0)),
            pl.BlockSpec((D_MODEL, 1), lambda bi, i, j: (0, 0)),
        ],
        out_specs=pl.BlockSpec((1, D_MODEL, ts), lambda bi, i, j: (bi, 0, i)),
        out_shape=jax.ShapeDtypeStruct(xT.shape, F32),
        scratch_shapes=[pltpu.VMEM((D_MODEL, ts), BF16), pltpu.VMEM((D_MODEL, ts), F32)],
        compiler_params=_cparams(("parallel", "parallel", "arbitrary")),
        name="ffn",
    )(xT, wguT, wdT, g, b)


def _inproj_kernel(x_ref, win_ref, gq_ref, gkv_ref, wuq_ref, wukv_ref,
                   c32_ref, s32_ref, c64_ref, s64_ref,
                   qa_ref, ka_ref, va_ref, qd_ref, kd_ref, vd_ref, qc_ref, kc_ref, vc_ref):
    xb = x_ref[0].astype(BF16)
    ts = xb.shape[-1]
    c32, s32 = c32_ref[...], s32_ref[...]
    c64, s64 = c64_ref[...], s64_ref[...]

    def proj(lo, hi):
        return jnp.dot(win_ref[lo:hi, :], xb, preferred_element_type=F32)

    nq = _rms_norm_rows(proj(IN_OFFS[0], IN_OFFS[1]), gq_ref[...]).astype(BF16)
    qa = jnp.dot(wuq_ref[...], nq, preferred_element_type=F32)
    nkv = _rms_norm_rows(proj(IN_OFFS[1], IN_OFFS[2]), gkv_ref[...]).astype(BF16)
    kva = jnp.dot(wukv_ref[...], nkv, preferred_element_type=F32)
    kpe = _rope_rows(proj(IN_OFFS[2], IN_OFFS[3]), c32, s32, 1, A_ROPE)
    qscale = (A_NOPE + A_ROPE) ** -0.5 * LOG2E
    zpad = jnp.zeros((A_QK_PAD - A_NOPE - A_ROPE, ts), F32)
    dqk = A_NOPE + A_ROPE
    for h in range(A_HEADS):
        qh = qa[h * dqk:(h + 1) * dqk]
        q_pe = _rope_rows(qh[A_NOPE:], c32, s32, 1, A_ROPE)
        qa_ref[0, h] = (jnp.concatenate([qh[:A_NOPE], q_pe, zpad], axis=0) * qscale).astype(BF16)
        kvh = kva[h * (A_NOPE + A_V):(h + 1) * (A_NOPE + A_V)]
        kT = jnp.concatenate([kvh[:A_NOPE], kpe, zpad], axis=0)
        ka_ref[0, h] = kT.T.astype(BF16)
        va_ref[0, h, 0] = kvh[A_NOPE:].astype(BF16)

    pb = proj(IN_OFFS[3], IN_OFFS[6])
    qd = _rope_rows(pb[0:256], c32, s32, 2 * B_HEADS, B_D) * (B_D ** -0.5 * LOG2E)
    qd_ref[0] = qd.astype(BF16)
    kd = _rope_rows(pb[256:512], c32, s32, 2 * B_HEADS, B_D)
    kd_ref[0] = kd.T.astype(BF16)
    for h in range(B_HEADS):
        vd_ref[0, h, 0] = pb[512 + h * B_V:512 + (h + 1) * B_V].astype(BF16)

    pc = proj(IN_OFFS[6], IN_OFFS[9])
    qc = _rope_rows(pc[0:256], c64, s64, C_HEADS, C_DIM) * (C_DIM ** -0.5 * LOG2E)
    kc = _rope_rows(pc[256:512], c64, s64, C_HEADS, C_DIM)
    qc_ref[0] = qc.T
    kc_ref[0] = kc.T
    vc_ref[0] = pc[512:768].T


def _inproj_layer(xT, winT, gq, gkv, wuqT, wukvT, rope):
    bsz, _, seq = xT.shape
    ts = SEQ_TILE
    nt = seq // ts
    c32, s32, c64, s64 = rope
    full = lambda shape: pl.BlockSpec(shape, lambda bi, i: (0,) * len(shape))
    tab = lambda rows: pl.BlockSpec((rows, ts), lambda bi, i: (0, i))
    out_shape = (
        jax.ShapeDtypeStruct((bsz, A_HEADS, A_QK_PAD, seq), BF16),
        jax.ShapeDtypeStruct((bsz, A_HEADS, seq, A_QK_PAD), BF16),
        jax.ShapeDtypeStruct((bsz, A_HEADS, nt, A_V, ts), BF16),
        jax.ShapeDtypeStruct((bsz, 2 * B_HEADS * B_D, seq), BF16),
        jax.ShapeDtypeStruct((bsz, seq, 2 * B_HEADS * B_D), BF16),
        jax.ShapeDtypeStruct((bsz, B_HEADS, nt, B_V, ts), BF16),
        jax.ShapeDtypeStruct((bsz, seq, C_HEADS * C_DIM), F32),
        jax.ShapeDtypeStruct((bsz, seq, C_HEADS * C_DIM), F32),
        jax.ShapeDtypeStruct((bsz, seq, C_HEADS * C_DIM), F32),
    )
    out_specs = (
        pl.BlockSpec((1, A_HEADS, A_QK_PAD, ts), lambda bi, i: (bi, 0, 0, i)),
        pl.BlockSpec((1, A_HEADS, ts, A_QK_PAD), lambda bi, i: (bi, 0, i, 0)),
        pl.BlockSpec((1, A_HEADS, 1, A_V, ts), lambda bi, i: (bi, 0, i, 0, 0)),
        pl.BlockSpec((1, 256, ts), lambda bi, i: (bi, 0, i)),
        pl.BlockSpec((1, ts, 256), lambda bi, i: (bi, i, 0)),
        pl.BlockSpec((1, B_HEADS, 1, B_V, ts), lambda bi, i: (bi, 0, i, 0, 0)),
        pl.BlockSpec((1, ts, 256), lambda bi, i: (bi, i, 0)),
        pl.BlockSpec((1, ts, 256), lambda bi, i: (bi, i, 0)),
        pl.BlockSpec((1, ts, 256), lambda bi, i: (bi, i, 0)),
    )
    return pl.pallas_call(
        _inproj_kernel,
        grid=(bsz, nt),
        in_specs=[
            pl.BlockSpec((1, D_MODEL, ts), lambda bi, i: (bi, 0, i)),
            full(winT.shape), full(gq.shape), full(gkv.shape), full(wuqT.shape), full(wukvT.shape),
            tab(A_ROPE // 2), tab(A_ROPE // 2), tab(C_DIM // 2), tab(C_DIM // 2),
        ],
        out_specs=out_specs,
        out_shape=out_shape,
        compiler_params=_cparams(("parallel", "parallel")),
        name="inproj",
    )(xT, winT, gq, gkv, wuqT, wukvT, c32, s32, c64, s64)


def _attn_kernel(q_ref, k_ref, v_ref, o_ref, s_ref, p_ref, *, tk, nk, group_rows):
    chains = s_ref.shape[1]
    q = q_ref[...].reshape(q_ref.shape[-2:])
    if group_rows:
        rows = lax.broadcasted_iota(jnp.int32, q.shape, 0)
        lo = pl.program_id(1) * group_rows
        q = jnp.where((rows >= lo) & (rows < lo + group_rows), q, jnp.zeros_like(q))
    tq = q.shape[-1]
    tc = tq // chains
    qs = [q[:, c * tc:(c + 1) * tc] for c in range(chains)]
    kv = k_ref.at[0] if len(k_ref.shape) == 3 else k_ref.at[0, 0]

    dv = v_ref.shape[-2]
    ones_rows = (lax.broadcasted_iota(jnp.int32, (SUM_ROWS, tk), 0) == 0).astype(BF16)

    def score_matmul(i, slot):
        start = pl.multiple_of(i * tk, tk)
        kc = kv[pl.ds(start, tk), :]
        cmax = []
        for c in range(chains):
            s = jnp.dot(kc, qs[c], preferred_element_type=F32)
            s_ref[slot, c] = s
            cmax.append(jnp.max(s, axis=0, keepdims=True))
        return tuple(cmax)

    def value_matmul(i, slot, carry):
        vc = jnp.concatenate([v_ref[0, 0, i], ones_rows], axis=0)
        pvs = [jnp.dot(vc, p_ref[slot, c], preferred_element_type=F32) for c in range(chains)]
        return tuple((m, alpha * acc + pvs[c], alpha) for c, (m, acc, alpha) in enumerate(carry))

    def step(i, carry, cmax, slot, first=False, last=False):
        cmax_next = None if last else score_matmul(i + 1, 1 - slot)
        if not first:
            carry = value_matmul(i - 1, 1 - slot, carry)
        out = []
        for c, (m, acc, _) in enumerate(carry):
            m_new = jnp.maximum(m, cmax[c])
            p_ref[slot, c] = jnp.exp2(s_ref[slot, c] - m_new).astype(BF16)
            out.append((m_new, acc, jnp.exp2(m - m_new)))
        return tuple(out), cmax_next

    def pair(j, state):
        i = 2 * j + 1
        carry, cmax = state
        carry, cmax = step(i, carry, cmax, 1)
        return step(i + 1, carry, cmax, 0)

    carry = tuple((jnp.full((1, tc), NEG_BIG, F32), jnp.zeros((dv + SUM_ROWS, tc), F32),
                   jnp.ones((1, tc), F32)) for _ in range(chains))
    cmax = score_matmul(0, 0)
    state = step(0, carry, cmax, 0, first=True)
    carry, cmax = lax.fori_loop(0, (nk - 2) // 2, pair, state)
    carry, _ = step(nk - 1, carry, cmax, 1, last=True)
    carry = value_matmul(nk - 1, 1, carry)
    for c, (_, acc, _) in enumerate(carry):
        o_ref[0, 0, :, c * tc:(c + 1) * tc] = acc[:dv] / acc[dv:dv + 1]


def _attention(qT, k, vT, *, heads, shared_qk, v_heads):
    bsz, seq = qT.shape[0], qT.shape[-1]
    nk, dv, tk = vT.shape[2], vT.shape[3], vT.shape[4]
    tq = ATTN_TQ
    vrep = heads // v_heads
    if shared_qk:
        dq_all = qT.shape[1]
        q_spec = pl.BlockSpec((1, dq_all, tq), lambda bi, h, i: (bi, 0, i))
        k_spec = pl.BlockSpec((1, seq, dq_all), lambda bi, h, i: (bi, 0, 0))
        group_rows = dq_all // heads
    else:
        dq = qT.shape[2]
        q_spec = pl.BlockSpec((1, 1, dq, tq), lambda bi, h, i: (bi, h, 0, i))
        k_spec = pl.BlockSpec((1, 1, seq, dq), lambda bi, h, i: (bi, h, 0, 0))
        group_rows = 0
    return pl.pallas_call(
        functools.partial(_attn_kernel, tk=tk, nk=nk, group_rows=group_rows),
        scratch_shapes=[pltpu.VMEM((2, ATTN_CHAINS, tk, tq // ATTN_CHAINS), F32),
                        pltpu.VMEM((2, ATTN_CHAINS, tk, tq // ATTN_CHAINS), BF16)],
        grid=(bsz, heads, seq // tq),
        in_specs=[
            q_spec, k_spec,
            pl.BlockSpec((1, 1, nk, dv, tk), lambda bi, h, i: (bi, h // vrep, 0, 0, 0)),
        ],
        out_specs=pl.BlockSpec((1, 1, dv, tq), lambda bi, h, i: (bi, h, 0, i)),
        out_shape=jax.ShapeDtypeStruct((bsz, heads, dv, seq), F32),
        compiler_params=_cparams(("parallel", "parallel", "arbitrary")),
        name="attn_shared" if shared_qk else "attn",
    )(qT, k, vT)


def _dilated_kernel(q_ref, k_ref, v_ref, o_ref, m_ref, l_ref, acc_ref, *, seq, sup):
    sb = pl.program_id(2)
    p0 = sb * sup
    m_ref[...] = jnp.full(m_ref.shape, NEG_BIG, F32)
    l_ref[...] = jnp.zeros(l_ref.shape, F32)
    acc_ref[...] = jnp.zeros(acc_ref.shape, F32)
    lane = lax.broadcasted_iota(jnp.int32, (1, 2 * C_DIM), 1)
    head_lanes = (lane < C_DIM, lane >= C_DIM)
    qi = lax.broadcasted_iota(jnp.int32, (DIL_QBLK, 1), 0)
    ki = lax.broadcasted_iota(jnp.int32, (1, DIL_KBLK), 1)

    for _, dil in C_BRANCHES:
        cls_len = seq // dil
        blocks = sup // (DIL_QBLK * dil)

        def body(tt, carry, dil=dil, cls_len=cls_len, blocks=blocks):
            blk = []
            for u in range(DIL_UNROLL):
                t = tt * DIL_UNROLL + u
                r = lax.div(t, blocks)
                jb = lax.rem(t, blocks)
                c0 = sb * (sup // dil) + DIL_QBLK * jb
                ks = jnp.clip(c0 - C_HALF, 0, cls_len - DIL_KBLK)
                qrow = r + dil * DIL_QBLK * jb
                ssl = pl.ds(qrow, DIL_QBLK, stride=dil)
                ksl = pl.ds(r + dil * ks, DIL_KBLK, stride=dil)
                blk.append(dict(
                    ssl=ssl,
                    qv=q_ref[0, pl.ds(p0 + qrow, DIL_QBLK, stride=dil), :],
                    kb=k_ref[0, ksl, :].astype(BF16),
                    vb=v_ref[0, ksl, :].astype(BF16),
                    band=jnp.abs((c0 + qi) - (ks + ki)) <= C_HALF,
                    acc=acc_ref[ssl, :],
                    m=[m_ref[hh, ssl, :] for hh in range(2)],
                    l=[l_ref[hh, ssl, :] for hh in range(2)]))
            for b in blk:
                b["s"] = [lax.dot_general(jnp.where(head_lanes[hh], b["qv"], 0.0).astype(BF16),
                                          b["kb"], (((1,), (1,)), ((), ())),
                                          preferred_element_type=F32) for hh in range(2)]
            for b in blk:
                b["p"], b["alpha"] = [], []
                for hh in range(2):
                    s = jnp.where(b["band"], b["s"][hh], NEG_BIG)
                    m_new = jnp.maximum(b["m"][hh], jnp.max(s, axis=1, keepdims=True))
                    p = jnp.exp2(s - m_new)
                    alpha = jnp.exp2(b["m"][hh] - m_new)
                    b["l"][hh] = alpha * b["l"][hh] + jnp.sum(p, axis=1, keepdims=True)
                    b["m"][hh] = m_new
                    b["p"].append(p.astype(BF16))
                    b["alpha"].append(alpha)
            for b in blk:
                pv = [jnp.dot(b["p"][hh], b["vb"], preferred_element_type=F32) for hh in range(2)]
                b["acc"] = jnp.where(head_lanes[0], b["alpha"][0] * b["acc"] + pv[0],
                                     b["alpha"][1] * b["acc"] + pv[1])
            for b in blk:
                acc_ref[b["ssl"], :] = b["acc"]
                for hh in range(2):
                    m_ref[hh, b["ssl"], :] = b["m"][hh]
                    l_ref[hh, b["ssl"], :] = b["l"][hh]
            return carry

        lax.fori_loop(0, dil * blocks // DIL_UNROLL, body, 0)

    denom = jnp.where(head_lanes[0], l_ref[0], l_ref[1])
    o_ref[0] = acc_ref[...] / denom


def _dilated_attention(qc, kc, vc):
    bsz, seq, width = qc.shape
    sup = min(DIL_SUPER, seq)
    pair = 2 * C_DIM
    in_spec = pl.BlockSpec((1, seq, pair), lambda bi, hp, sb: (bi, 0, hp))
    return pl.pallas_call(
        functools.partial(_dilated_kernel, seq=seq, sup=sup),
        grid=(bsz, width // pair, seq // sup),
        in_specs=[in_spec, in_spec, in_spec],
        out_specs=pl.BlockSpec((1, sup, pair), lambda bi, hp, sb: (bi, sb, hp)),
        out_shape=jax.ShapeDtypeStruct((bsz, seq, width), F32),
        scratch_shapes=[pltpu.VMEM((2, sup, 1), F32), pltpu.VMEM((2, sup, 1), F32),
                        pltpu.VMEM((sup, pair), F32)],
        compiler_params=_cparams(("parallel", "parallel", "arbitrary")),
        name="dilated",
    )(qc, kc, vc)


def _outproj_kernel(x_ref, oa_ref, od_ref, oc_ref, woa_ref, wob_ref, woc_ref, lam_ref, dg_ref,
                    g_ref, b_ref, o_ref, *, lambda_init):
    ts = x_ref.shape[-1]
    lf = lam_ref[...]
    lam = (jnp.exp(jnp.sum(lf[0:1] * lf[1:2], axis=1, keepdims=True))
           - jnp.exp(jnp.sum(lf[2:3] * lf[3:4], axis=1, keepdims=True)) + lambda_init)
    od = od_ref[0].reshape(B_HEADS, 2, B_V, ts)
    ob = od[:, 0] - lam * od[:, 1]
    ms = jnp.mean(ob * ob, axis=1, keepdims=True)
    ob = ob * lax.rsqrt(ms + NORM_EPS) * dg_ref[...] * (1.0 - lambda_init)
    ob = ob.reshape(B_HEADS * B_V, ts).astype(BF16)
    y = jnp.dot(woa_ref[...], oa_ref[0].astype(BF16), preferred_element_type=F32)
    y += jnp.dot(wob_ref[...], ob, preferred_element_type=F32)
    y += lax.dot_general(woc_ref[...], oc_ref[0].astype(BF16), (((1,), (1,)), ((), ())),
                         preferred_element_type=F32)
    o_ref[0] = _layer_norm_rows(ALPHA * x_ref[0] + y, g_ref[...], b_ref[...])


def _outproj_layer(xT, oaT, odT, oc, woaT, wobT, wocT, lam, dg, g, b, lambda_init):
    bsz, _, seq = xT.shape
    ts = SEQ_TILE
    full = lambda a: pl.BlockSpec(a.shape, lambda bi, i: (0,) * a.ndim)
    col = lambda rows: pl.BlockSpec((1, rows, ts), lambda bi, i: (bi, 0, i))
    return pl.pallas_call(
        functools.partial(_outproj_kernel, lambda_init=lambda_init),
        grid=(bsz, seq // ts),
        in_specs=[
            col(D_MODEL), col(A_HEADS * A_V), col(2 * B_HEADS * B_V),
            pl.BlockSpec((1, ts, C_HEADS * C_DIM), lambda bi, i: (bi, i, 0)),
            full(woaT), full(wobT), full(wocT), full(lam), full(dg), full(g), full(b),
        ],
        out_specs=col(D_MODEL),
        out_shape=jax.ShapeDtypeStruct(xT.shape, F32),
        compiler_params=_cparams(("parallel", "parallel")),
        name="outproj",
    )(xT, oaT, odT, oc, woaT, wobT, wocT, lam, dg, g, b)


def _rope_tables(seq, dim):
    inv = 1.0 / (ROPE_THETA ** (jnp.arange(0, dim, 2, dtype=F32) / dim))
    ang = inv[:, None] * jnp.arange(seq, dtype=F32)[None, :]
    return jnp.cos(ang), jnp.sin(ang)


def _prep_layer(i, ln_g, ln_b, ffn_w_gate, ffn_w_up, ffn_w_down, w_in, mla_q_norm, mla_kv_norm,
                mla_w_uq, mla_w_ukv, diff_lambda, diff_subln, w_out):
    tf = D_FF // FF_CHUNKS
    col = lambda v: v.astype(F32)[:, None]
    ffn = []
    for s in range(2):
        wg = ffn_w_gate[i, s].T.reshape(FF_CHUNKS, tf, D_MODEL)
        wu = ffn_w_up[i, s].T.reshape(FF_CHUNKS, tf, D_MODEL)
        wgu = jnp.concatenate([wg, wu], axis=1).astype(BF16)
        wd = ffn_w_down[i, s].T.reshape(D_MODEL, FF_CHUNKS, tf).transpose(1, 0, 2).astype(BF16)
        ffn.append((wgu, wd))
    woT = w_out[i].T.astype(BF16)
    na, nb = A_HEADS * A_V, B_HEADS * B_V
    return dict(
        ffn=ffn,
        ln=[(col(ln_g[i, s]), col(ln_b[i, s])) for s in range(3)],
        winT=w_in[i].T.astype(BF16),
        gq=col(mla_q_norm[i]), gkv=col(mla_kv_norm[i]),
        wuqT=mla_w_uq[i].T.astype(BF16), wukvT=mla_w_ukv[i].T.astype(BF16),
        woaT=woT[:, :na], wobT=woT[:, na:na + nb], wocT=woT[:, na + nb:],
        lam=diff_lambda[i].astype(F32), dg=col(diff_subln[i]),
    )


def _run(x, layers):
    bsz, seq, _ = x.shape
    assert seq % DIL_SUPER == 0 and seq // C_BRANCHES[-1][1] >= DIL_KBLK, seq
    assert seq % SEQ_TILE == 0 and seq % ATTN_TQ == 0 and FF_CHUNKS == 2
    rope = _rope_tables(seq, A_ROPE) + _rope_tables(seq, C_DIM)
    xT = jnp.swapaxes(x, 1, 2)
    for i, w in enumerate(layers):
        lambda_init = 0.8 - 0.6 * math.exp(-0.3 * i)
        xT = _ffn_layer(xT, *w["ffn"][0], *w["ln"][0])
        qaT, ka, vaT, qdT, kd, vdT, qc, kc, vc = _inproj_layer(
            xT, w["winT"], w["gq"], w["gkv"], w["wuqT"], w["wukvT"], rope)
        oaT = _attention(qaT, ka, vaT, heads=A_HEADS, shared_qk=False, v_heads=A_HEADS)
        odT = _attention(qdT, kd, vdT, heads=2 * B_HEADS, shared_qk=True, v_heads=B_HEADS)
        oc = _dilated_attention(qc, kc, vc)
        xT = _outproj_layer(xT, oaT.reshape(bsz, A_HEADS * A_V, seq),
                            odT.reshape(bsz, 2 * B_HEADS * B_V, seq), oc,
                            w["woaT"], w["wobT"], w["wocT"], w["lam"], w["dg"], *w["ln"][1],
                            lambda_init)
        xT = _ffn_layer(xT, *w["ffn"][1], *w["ln"][2])
    return jnp.swapaxes(xT, 1, 2)


def kernel(x_prompt, x_sample, ln_g, ln_b, ffn_w_gate, ffn_w_up, ffn_w_down, w_in, mla_q_norm,
           mla_kv_norm, mla_w_uq, mla_w_ukv, diff_lambda, diff_subln, w_out):
    params = (ln_g, ln_b, ffn_w_gate, ffn_w_up, ffn_w_down, w_in, mla_q_norm, mla_kv_norm,
              mla_w_uq, mla_w_ukv, diff_lambda, diff_subln, w_out)
    layers = [_prep_layer(i, *params) for i in range(DEPTH)]
    return _run(x_prompt, layers), _run(x_sample, layers)
```

```python
import functools
import math

import jax
import jax.numpy as jnp
from jax import lax
from jax.experimental import pallas as pl
from jax.experimental.pallas import tpu as pltpu

F32 = jnp.float32
BF16 = jnp.bfloat16

D_MODEL = 1024
D_FF = 2816
DEPTH = 4
ROPE_THETA = 10000.0
NORM_EPS = 1e-5
A_HEADS, A_NOPE, A_ROPE, A_V = 8, 64, 32, 64
A_Q_LORA, A_KV_LORA = 384, 256
A_QK_PAD = 128
B_HEADS, B_D = 4, 32
B_V = 2 * B_D
C_HEADS, C_DIM = 4, 64
C_BRANCHES = ((128, 1), (512, 4), (2048, 16))
C_HALF = 64
IN_SPLITS = (A_Q_LORA, A_KV_LORA, A_ROPE, 256, 256, 256, 256, 256, 256)
IN_OFFS = tuple(int(sum(IN_SPLITS[:i])) for i in range(len(IN_SPLITS) + 1))
ALPHA = (2 * DEPTH) ** 0.25
LOG2E = math.log2(math.e)

VMEM_LIMIT_BYTES = 56 * 1024 * 1024
SEQ_TILE = 512
FF_CHUNKS = 2
ATTN_TQ = 1024
ATTN_TK = 512
ATTN_CHAINS = 4
ATTN_UNROLL = 2
PIPE_SLOTS = 2
SUM_ROWS = 16
DIL_QBLK = 128
DIL_KBLK = 256
DIL_SUPER = 2048
DIL_UNROLL = 4
NEG_BIG = -1e30


def _cparams(sem):
    return pltpu.CompilerParams(dimension_semantics=sem, vmem_limit_bytes=VMEM_LIMIT_BYTES)


def _layer_norm_rows(y, g, b):
    mu = jnp.mean(y, axis=0, keepdims=True)
    d = y - mu
    var = jnp.mean(d * d, axis=0, keepdims=True)
    return d * lax.rsqrt(var + NORM_EPS) * g + b


def _rms_norm_rows(y, g):
    ms = jnp.mean(y * y, axis=0, keepdims=True)
    return y * lax.rsqrt(ms + NORM_EPS) * g


def _rope_rows(x, c, s, groups, dim):
    half = dim // 2
    t = x.shape[-1]
    xg = x.reshape(groups, 2, half, t)
    x1, x2 = xg[:, 0], xg[:, 1]
    o1 = x1 * c - x2 * s
    o2 = x2 * c + x1 * s
    return jnp.stack([o1, o2], axis=1).reshape(groups * dim, t)


def _ffn_kernel(x_ref, wgu_ref, wd_ref, g_ref, b_ref, o_ref, xb_ref, acc_ref, y_ref, *, tf):
    t, j = pl.program_id(0), pl.program_id(1)

    def swiglu_chunk(xb):
        gu = jnp.dot(wgu_ref[0], xb, preferred_element_type=F32)
        gate, up = gu[:tf], gu[tf:]
        h = (gate * jax.nn.sigmoid(gate) * up).astype(BF16)
        return jnp.dot(wd_ref[0], h, preferred_element_type=F32)

    @pl.when((t == 0) & (j == 0))
    def _():
        y_ref[...] = jnp.zeros(y_ref.shape, F32)

    @pl.when(j == 0)
    def _():
        xb = x_ref[0].astype(BF16)
        xb_ref[...] = xb
        acc_ref[...] = swiglu_chunk(xb)
        o_ref[0] = _layer_norm_rows(y_ref[...], g_ref[...], b_ref[...])

    @pl.when(j == FF_CHUNKS - 1)
    def _():
        y_ref[...] = ALPHA * x_ref[0] + 0.5 * (acc_ref[...] + swiglu_chunk(xb_ref[...]))


def _ffn_layer(xT, wguT, wdT, g, b):
    bsz, _, seq = xT.shape
    ts = SEQ_TILE
    tf = D_FF // FF_CHUNKS
    nt = seq // ts
    tiles = bsz * nt

    def x_map(t, j):
        tt = jnp.minimum(t, tiles - 1)
        return (tt // nt, 0, tt % nt)

    def o_map(t, j):
        tt = jnp.maximum(t - 1, 0)
        return (tt // nt, 0, tt % nt)

    return pl.pallas_call(
        functools.partial(_ffn_kernel, tf=tf),
        grid=(tiles + 1, FF_CHUNKS),
        in_specs=[
            pl.BlockSpec((1, D_MODEL, ts), x_map),
            pl.BlockSpec((1, 2 * tf, D_MODEL), lambda t, j: (j, 0, 0)),
            pl.BlockSpec((1, D_MODEL, tf), lambda t, j: (j, 0, 0)),
            pl.BlockSpec((D_MODEL, 1), lambda t, j: (0, 0)),
            pl.BlockSpec((D_MODEL, 1), lambda t, j: (0, 0)),
        ],
        out_specs=pl.BlockSpec((1, D_MODEL, ts), o_map),
        out_shape=jax.ShapeDtypeStruct(xT.shape, F32),
        scratch_shapes=[pltpu.VMEM((D_MODEL, ts), BF16), pltpu.VMEM((D_MODEL, ts), F32),
                        pltpu.VMEM((D_MODEL, ts), F32)],
        compiler_params=_cparams(("arbitrary", "arbitrary")),
        name="ffn",
    )(xT, wguT, wdT, g, b)


def _inproj_kernel(x_ref, win_ref, gq_ref, gkv_ref, wuq_ref, wukv_ref,
                   c32_ref, s32_ref, c64_ref, s64_ref,
                   qa_ref, ka_ref, va_ref, qd_ref, kd_ref, vd_ref, qc_ref, kc_ref, vc_ref):
    xb = x_ref[0].astype(BF16)
    ts = xb.shape[-1]
    c32, s32 = c32_ref[...], s32_ref[...]
    c64, s64 = c64_ref[...], s64_ref[...]

    def proj(lo, hi):
        return jnp.dot(win_ref[lo:hi, :], xb, preferred_element_type=F32)

    nq = _rms_norm_rows(proj(IN_OFFS[0], IN_OFFS[1]), gq_ref[...]).astype(BF16)
    qa = jnp.dot(wuq_ref[...], nq, preferred_element_type=F32)
    nkv = _rms_norm_rows(proj(IN_OFFS[1], IN_OFFS[2]), gkv_ref[...]).astype(BF16)
    kva = jnp.dot(wukv_ref[...], nkv, preferred_element_type=F32)
    kpe = _rope_rows(proj(IN_OFFS[2], IN_OFFS[3]), c32, s32, 1, A_ROPE)
    qscale = (A_NOPE + A_ROPE) ** -0.5 * LOG2E
    zpad = jnp.zeros((A_QK_PAD - A_NOPE - A_ROPE, ts), F32)
    dqk = A_NOPE + A_ROPE
    for h in range(A_HEADS):
        qh = qa[h * dqk:(h + 1) * dqk]
        q_pe = _rope_rows(qh[A_NOPE:], c32, s32, 1, A_ROPE)
        qa_ref[0, h] = (jnp.concatenate([qh[:A_NOPE], q_pe, zpad], axis=0) * qscale).astype(BF16)
        kvh = kva[h * (A_NOPE + A_V):(h + 1) * (A_NOPE + A_V)]
        kT = jnp.concatenate([kvh[:A_NOPE], kpe, zpad], axis=0)
        ka_ref[0, h] = kT.T.astype(BF16)
        va_ref[0, h, 0] = kvh[A_NOPE:].astype(BF16)

    pb = proj(IN_OFFS[3], IN_OFFS[6])
    qd = _rope_rows(pb[0:256], c32, s32, 2 * B_HEADS, B_D) * (B_D ** -0.5 * LOG2E)
    qd_ref[0] = qd.astype(BF16)
    kd = _rope_rows(pb[256:512], c32, s32, 2 * B_HEADS, B_D)
    kd_ref[0] = kd.T.astype(BF16)
    for h in range(B_HEADS):
        vd_ref[0, h, 0] = pb[512 + h * B_V:512 + (h + 1) * B_V].astype(BF16)

    pc = proj(IN_OFFS[6], IN_OFFS[9])
    qc = _rope_rows(pc[0:256], c64, s64, C_HEADS, C_DIM) * (C_DIM ** -0.5 * LOG2E)
    kc = _rope_rows(pc[256:512], c64, s64, C_HEADS, C_DIM)
    qc_ref[0] = qc.T
    kc_ref[0] = kc.T
    vc_ref[0] = pc[512:768].T


def _inproj_layer(xT, winT, gq, gkv, wuqT, wukvT, rope):
    bsz, _, seq = xT.shape
    ts = SEQ_TILE
    nt = seq // ts
    c32, s32, c64, s64 = rope
    full = lambda shape: pl.BlockSpec(shape, lambda bi, i: (0,) * len(shape))
    tab = lambda rows: pl.BlockSpec((rows, ts), lambda bi, i: (0, i))
    out_shape = (
        jax.ShapeDtypeStruct((bsz, A_HEADS, A_QK_PAD, seq), BF16),
        jax.ShapeDtypeStruct((bsz, A_HEADS, seq, A_QK_PAD), BF16),
        jax.ShapeDtypeStruct((bsz, A_HEADS, nt, A_V, ts), BF16),
        jax.ShapeDtypeStruct((bsz, 2 * B_HEADS * B_D, seq), BF16),
        jax.ShapeDtypeStruct((bsz, seq, 2 * B_HEADS * B_D), BF16),
        jax.ShapeDtypeStruct((bsz, B_HEADS, nt, B_V, ts), BF16),
        jax.ShapeDtypeStruct((bsz, seq, C_HEADS * C_DIM), F32),
        jax.ShapeDtypeStruct((bsz, seq, C_HEADS * C_DIM), F32),
        jax.ShapeDtypeStruct((bsz, seq, C_HEADS * C_DIM), F32),
    )
    out_specs = (
        pl.BlockSpec((1, A_HEADS, A_QK_PAD, ts), lambda bi, i: (bi, 0, 0, i)),
        pl.BlockSpec((1, A_HEADS, ts, A_QK_PAD), lambda bi, i: (bi, 0, i, 0)),
        pl.BlockSpec((1, A_HEADS, 1, A_V, ts), lambda bi, i: (bi, 0, i, 0, 0)),
        pl.BlockSpec((1, 256, ts), lambda bi, i: (bi, 0, i)),
        pl.BlockSpec((1, ts, 256), lambda bi, i: (bi, i, 0)),
        pl.BlockSpec((1, B_HEADS, 1, B_V, ts), lambda bi, i: (bi, 0, i, 0, 0)),
        pl.BlockSpec((1, ts, 256), lambda bi, i: (bi, i, 0)),
        pl.BlockSpec((1, ts, 256), lambda bi, i: (bi, i, 0)),
        pl.BlockSpec((1, ts, 256), lambda bi, i: (bi, i, 0)),
    )
    return pl.pallas_call(
        _inproj_kernel,
        grid=(bsz, nt),
        in_specs=[
            pl.BlockSpec((1, D_MODEL, ts), lambda bi, i: (bi, 0, i)),
            full(winT.shape), full(gq.shape), full(gkv.shape), full(wuqT.shape), full(wukvT.shape),
            tab(A_ROPE // 2), tab(A_ROPE // 2), tab(C_DIM // 2), tab(C_DIM // 2),
        ],
        out_specs=out_specs,
        out_shape=out_shape,
        compiler_params=_cparams(("parallel", "parallel")),
        name="inproj",
    )(xT, winT, gq, gkv, wuqT, wukvT, c32, s32, c64, s64)


def _attn_kernel(q_ref, k_ref, v_ref, o_ref, s_ref, p_ref, *, tk, nk, group_rows):
    chains = s_ref.shape[1]
    q = q_ref[...].reshape(q_ref.shape[-2:])
    if group_rows:
        rows = lax.broadcasted_iota(jnp.int32, q.shape, 0)
        lo = pl.program_id(1) * group_rows
        q = jnp.where((rows >= lo) & (rows < lo + group_rows), q, jnp.zeros_like(q))
    tq = q.shape[-1]
    tc = tq // chains
    qs = [q[:, c * tc:(c + 1) * tc] for c in range(chains)]
    kv = k_ref.at[0] if len(k_ref.shape) == 3 else k_ref.at[0, 0]

    dv = v_ref.shape[-2]
    ones_rows = (lax.broadcasted_iota(jnp.int32, (SUM_ROWS, tk), 0) == 0).astype(BF16)

    def score_matmul(i, slot):
        start = pl.multiple_of(i * tk, tk)
        kc = kv[pl.ds(start, tk), :]
        cmax = []
        for c in range(chains):
            s = jnp.dot(kc, qs[c], preferred_element_type=F32)
            s_ref[slot, c] = s
            cmax.append(jnp.max(s, axis=0, keepdims=True))
        return tuple(cmax)

    vsub = v_ref.shape[-1] // tk
    assert PIPE_SLOTS % vsub == 0

    def value_matmul(i, slot, carry):
        part = slot % vsub
        blk = i // vsub if isinstance(i, int) else lax.div(i, vsub)
        vc = jnp.concatenate([v_ref[0, 0, blk, :, part * tk:(part + 1) * tk], ones_rows],
                             axis=0)
        pvs = [jnp.dot(vc, p_ref[slot, c], preferred_element_type=F32) for c in range(chains)]
        return tuple((m, alpha * acc + pvs[c], alpha) for c, (m, acc, alpha) in enumerate(carry))

    def step(i, carry, cmax, slot, first=False, last=False):
        nxt, prv = (slot + 1) % PIPE_SLOTS, (slot - 1) % PIPE_SLOTS
        cmax_next = None if last else score_matmul(i + 1, nxt)
        if not first:
            carry = value_matmul(i - 1, prv, carry)
        out = []
        for c, (m, acc, _) in enumerate(carry):
            m_new = jnp.maximum(m, cmax[c])
            p_ref[slot, c] = jnp.exp2(s_ref[slot, c] - m_new).astype(BF16)
            out.append((m_new, acc, jnp.exp2(m - m_new)))
        return tuple(out), cmax_next

    assert ATTN_UNROLL % PIPE_SLOTS == 0
    lead = 2 + (nk - 4) % ATTN_UNROLL

    def steps(j, state):
        carry, cmax = state
        for u in range(ATTN_UNROLL):
            carry, cmax = step(ATTN_UNROLL * j + lead + u, carry, cmax, (lead + u) % PIPE_SLOTS)
        return carry, cmax

    carry = tuple((jnp.full((1, tc), NEG_BIG, F32), jnp.zeros((dv + SUM_ROWS, tc), F32),
                   jnp.ones((1, tc), F32)) for _ in range(chains))
    cmax = score_matmul(0, 0)
    for i in range(lead):
        carry, cmax = step(i, carry, cmax, i % PIPE_SLOTS, first=(i == 0))
    carry, cmax = lax.fori_loop(0, (nk - 2 - lead) // ATTN_UNROLL, steps, (carry, cmax))
    carry, cmax = step(nk - 2, carry, cmax, (nk - 2) % PIPE_SLOTS)
    carry, _ = step(nk - 1, carry, cmax, (nk - 1) % PIPE_SLOTS, last=True)
    carry = value_matmul(nk - 1, (nk - 1) % PIPE_SLOTS, carry)
    for c, (_, acc, _) in enumerate(carry):
        o_ref[0, 0, :, c * tc:(c + 1) * tc] = acc[:dv] / acc[dv:dv + 1]


def _attention(qT, k, vT, *, heads, shared_qk, v_heads):
    bsz, seq = qT.shape[0], qT.shape[-1]
    nv, dv, tv = vT.shape[2], vT.shape[3], vT.shape[4]
    tq, tk = ATTN_TQ, ATTN_TK
    nk = seq // tk
    vrep = heads // v_heads
    if shared_qk:
        dq_all = qT.shape[1]
        q_spec = pl.BlockSpec((1, dq_all, tq), lambda bi, h, i: (bi, 0, i))
        k_spec = pl.BlockSpec((1, seq, dq_all), lambda bi, h, i: (bi, 0, 0))
        group_rows = dq_all // heads
    else:
        dq = qT.shape[2]
        q_spec = pl.BlockSpec((1, 1, dq, tq), lambda bi, h, i: (bi, h, 0, i))
        k_spec = pl.BlockSpec((1, 1, seq, dq), lambda bi, h, i: (bi, h, 0, 0))
        group_rows = 0
    return pl.pallas_call(
        functools.partial(_attn_kernel, tk=tk, nk=nk, group_rows=group_rows),
        scratch_shapes=[pltpu.VMEM((PIPE_SLOTS, ATTN_CHAINS, tk, tq // ATTN_CHAINS), F32),
                        pltpu.VMEM((PIPE_SLOTS, ATTN_CHAINS, tk, tq // ATTN_CHAINS), BF16)],
        grid=(bsz, heads, seq // tq),
        in_specs=[
            q_spec, k_spec,
            pl.BlockSpec((1, 1, nv, dv, tv), lambda bi, h, i: (bi, h // vrep, 0, 0, 0)),
        ],
        out_specs=pl.BlockSpec((1, 1, dv, tq), lambda bi, h, i: (bi, h, 0, i)),
        out_shape=jax.ShapeDtypeStruct((bsz, heads, dv, seq), F32),
        compiler_params=_cparams(("parallel", "parallel", "arbitrary")),
        name="attn_shared" if shared_qk else "attn",
    )(qT, k, vT)


def _dilated_kernel(q_ref, k_ref, v_ref, o_ref, m_ref, l_ref, acc_ref, *, seq, sup):
    sb = pl.program_id(2)
    p0 = sb * sup
    m_ref[...] = jnp.full(m_ref.shape, NEG_BIG, F32)
    l_ref[...] = jnp.zeros(l_ref.shape, F32)
    acc_ref[...] = jnp.zeros(acc_ref.shape, F32)
    pair = 2 * C_DIM
    lane = lax.broadcasted_iota(jnp.int32, (1, pair), 1)
    h0 = lane < C_DIM
    qi = lax.broadcasted_iota(jnp.int32, (DIL_QBLK, 1), 0)
    ki = lax.broadcasted_iota(jnp.int32, (1, DIL_KBLK), 1)
    ones0 = jnp.broadcast_to(h0.astype(F32), (DIL_KBLK, pair))
    ones1 = 1.0 - ones0

    for _, dil in C_BRANCHES:
        cls_len = seq // dil
        blocks = sup // (DIL_QBLK * dil)

        def body(tt, carry, dil=dil, cls_len=cls_len, blocks=blocks):
            blk = []
            for u in range(DIL_UNROLL):
                t = tt * DIL_UNROLL + u
                r = lax.div(t, blocks)
                jb = lax.rem(t, blocks)
                c0 = sb * (sup // dil) + DIL_QBLK * jb
                ks = jnp.clip(c0 - C_HALF, 0, cls_len - DIL_KBLK)
                qrow = r + dil * DIL_QBLK * jb
                ssl = pl.ds(qrow, DIL_QBLK, stride=dil)
                ksl = pl.ds(r + dil * ks, DIL_KBLK, stride=dil)
                kf = k_ref[0, ksl, :]
                vf = v_ref[0, ksl, :]
                kcat = jnp.concatenate([jnp.where(h0, kf, 0.0), jnp.where(h0, 0.0, kf)], axis=0)
                vaug = jnp.concatenate(
                    [jnp.concatenate([jnp.where(h0, vf, 0.0), ones0], axis=1),
                     jnp.concatenate([jnp.where(h0, 0.0, vf), ones1], axis=1)], axis=0)
                blk.append(dict(
                    ssl=ssl,
                    q=q_ref[0, pl.ds(p0 + qrow, DIL_QBLK, stride=dil), :].astype(BF16),
                    kcat=kcat.astype(BF16),
                    vaug=vaug.astype(BF16),
                    band=jnp.abs((c0 + qi) - (ks + ki)) <= C_HALF,
                    acc=acc_ref[ssl, :], m=m_ref[ssl, :], l=l_ref[ssl, :]))
            for b in blk:
                b["s"] = lax.dot_general(b["q"], b["kcat"], (((1,), (1,)), ((), ())),
                                         preferred_element_type=F32)
            for b in blk:
                s0 = jnp.where(b["band"], b["s"][:, :DIL_KBLK], NEG_BIG)
                s1 = jnp.where(b["band"], b["s"][:, DIL_KBLK:], NEG_BIG)
                t0 = jnp.maximum(s0[:, :pair], s0[:, pair:])
                t1 = jnp.maximum(s1[:, :pair], s1[:, pair:])
                m0 = jnp.max(jnp.where(h0, jnp.maximum(t0, b["m"]), t0), axis=1, keepdims=True)
                m1 = jnp.max(jnp.where(h0, t1, jnp.maximum(t1, b["m"])), axis=1, keepdims=True)
                m_new = jnp.where(h0, m0, m1)
                b["p"] = jnp.concatenate([jnp.exp2(s0 - m0), jnp.exp2(s1 - m1)],
                                         axis=1).astype(BF16)
                b["alpha"] = jnp.exp2(b["m"] - m_new)
                b["m"] = m_new
            for b in blk:
                pv = jnp.dot(b["p"], b["vaug"], preferred_element_type=F32)
                b["acc"] = b["alpha"] * b["acc"] + pv[:, :pair]
                b["l"] = b["alpha"] * b["l"] + pv[:, pair:]
            for b in blk:
                acc_ref[b["ssl"], :] = b["acc"]
                m_ref[b["ssl"], :] = b["m"]
                l_ref[b["ssl"], :] = b["l"]
            return carry

        lax.fori_loop(0, dil * blocks // DIL_UNROLL, body, 0)

    o_ref[0] = acc_ref[...] / l_ref[...]


def _dilated_attention(qc, kc, vc):
    bsz, seq, width = qc.shape
    sup = min(DIL_SUPER, seq)
    pair = 2 * C_DIM
    in_spec = pl.BlockSpec((1, seq, pair), lambda bi, hp, sb: (bi, 0, hp))
    return pl.pallas_call(
        functools.partial(_dilated_kernel, seq=seq, sup=sup),
        grid=(bsz, width // pair, seq // sup),
        in_specs=[in_spec, in_spec, in_spec],
        out_specs=pl.BlockSpec((1, sup, pair), lambda bi, hp, sb: (bi, sb, hp)),
        out_shape=jax.ShapeDtypeStruct((bsz, seq, width), F32),
        scratch_shapes=[pltpu.VMEM((sup, pair), F32)] * 3,
        compiler_params=_cparams(("parallel", "parallel", "arbitrary")),
        name="dilated",
    )(qc, kc, vc)


def _outproj_kernel(x_ref, oa_ref, od_ref, oc_ref, woa_ref, wob_ref, woc_ref, lam_ref, dg_ref,
                    g_ref, b_ref, o_ref, *, lambda_init):
    ts = x_ref.shape[-1]
    lf = lam_ref[...]
    lam = (jnp.exp(jnp.sum(lf[0:1] * lf[1:2], axis=1, keepdims=True))
           - jnp.exp(jnp.sum(lf[2:3] * lf[3:4], axis=1, keepdims=True)) + lambda_init)
    od = od_ref[0].reshape(B_HEADS, 2, B_V, ts)
    ob = od[:, 0] - lam * od[:, 1]
    ms = jnp.mean(ob * ob, axis=1, keepdims=True)
    ob = ob * lax.rsqrt(ms + NORM_EPS) * dg_ref[...] * (1.0 - lambda_init)
    ob = ob.reshape(B_HEADS * B_V, ts).astype(BF16)
    y = jnp.dot(woa_ref[...], oa_ref[0].astype(BF16), preferred_element_type=F32)
    y += jnp.dot(wob_ref[...], ob, preferred_element_type=F32)
    y += lax.dot_general(woc_ref[...], oc_ref[0].astype(BF16), (((1,), (1,)), ((), ())),
                         preferred_element_type=F32)
    o_ref[0] = _layer_norm_rows(ALPHA * x_ref[0] + y, g_ref[...], b_ref[...])


def _outproj_layer(xT, oaT, odT, oc, woaT, wobT, wocT, lam, dg, g, b, lambda_init):
    bsz, _, seq = xT.shape
    ts = SEQ_TILE
    full = lambda a: pl.BlockSpec(a.shape, lambda bi, i: (0,) * a.ndim)
    col = lambda rows: pl.BlockSpec((1, rows, ts), lambda bi, i: (bi, 0, i))
    return pl.pallas_call(
        functools.partial(_outproj_kernel, lambda_init=lambda_init),
        grid=(bsz, seq // ts),
        in_specs=[
            col(D_MODEL), col(A_HEADS * A_V), col(2 * B_HEADS * B_V),
            pl.BlockSpec((1, ts, C_HEADS * C_DIM), lambda bi, i: (bi, i, 0)),
            full(woaT), full(wobT), full(wocT), full(lam), full(dg), full(g), full(b),
        ],
        out_specs=col(D_MODEL),
        out_shape=jax.ShapeDtypeStruct(xT.shape, F32),
        compiler_params=_cparams(("parallel", "parallel")),
        name="outproj",
    )(xT, oaT, odT, oc, woaT, wobT, wocT, lam, dg, g, b)


def _rope_tables(seq, dim):
    inv = 1.0 / (ROPE_THETA ** (jnp.arange(0, dim, 2, dtype=F32) / dim))
    ang = inv[:, None] * jnp.arange(seq, dtype=F32)[None, :]
    return jnp.cos(ang), jnp.sin(ang)


def _prep_layer(i, ln_g, ln_b, ffn_w_gate, ffn_w_up, ffn_w_down, w_in, mla_q_norm, mla_kv_norm,
                mla_w_uq, mla_w_ukv, diff_lambda, diff_subln, w_out):
    tf = D_FF // FF_CHUNKS
    col = lambda v: v.astype(F32)[:, None]
    ffn = []
    for s in range(2):
        wg = ffn_w_gate[i, s].T.reshape(FF_CHUNKS, tf, D_MODEL)
        wu = ffn_w_up[i, s].T.reshape(FF_CHUNKS, tf, D_MODEL)
        wgu = jnp.concatenate([wg, wu], axis=1).astype(BF16)
        wd = ffn_w_down[i, s].T.reshape(D_MODEL, FF_CHUNKS, tf).transpose(1, 0, 2).astype(BF16)
        ffn.append((wgu, wd))
    woT = w_out[i].T.astype(BF16)
    na, nb = A_HEADS * A_V, B_HEADS * B_V
    return dict(
        ffn=ffn,
        ln=[(col(ln_g[i, s]), col(ln_b[i, s])) for s in range(3)],
        winT=w_in[i].T.astype(BF16),
        gq=col(mla_q_norm[i]), gkv=col(mla_kv_norm[i]),
        wuqT=mla_w_uq[i].T.astype(BF16), wukvT=mla_w_ukv[i].T.astype(BF16),
        woaT=woT[:, :na], wobT=woT[:, na:na + nb], wocT=woT[:, na + nb:],
        lam=diff_lambda[i].astype(F32), dg=col(diff_subln[i]),
    )


def _run(x, layers):
    bsz, seq, _ = x.shape
    assert seq % DIL_SUPER == 0 and seq // C_BRANCHES[-1][1] >= DIL_KBLK, seq
    assert seq % SEQ_TILE == 0 and seq % ATTN_TQ == 0 and FF_CHUNKS == 2
    rope = _rope_tables(seq, A_ROPE) + _rope_tables(seq, C_DIM)
    xT = jnp.swapaxes(x, 1, 2)
    for i, w in enumerate(layers):
        lambda_init = 0.8 - 0.6 * math.exp(-0.3 * i)
        xT = _ffn_layer(xT, *w["ffn"][0], *w["ln"][0])
        qaT, ka, vaT, qdT, kd, vdT, qc, kc, vc = _inproj_layer(
            xT, w["winT"], w["gq"], w["gkv"], w["wuqT"], w["wukvT"], rope)
        oaT = _attention(qaT, ka, vaT, heads=A_HEADS, shared_qk=False, v_heads=A_HEADS)
        odT = _attention(qdT, kd, vdT, heads=2 * B_HEADS, shared_qk=True, v_heads=B_HEADS)
        oc = _dilated_attention(qc, kc, vc)
        xT = _outproj_layer(xT, oaT.reshape(bsz, A_HEADS * A_V, seq),
                            odT.reshape(bsz, 2 * B_HEADS * B_V, seq), oc,
                            w["woaT"], w["wobT"], w["wocT"], w["lam"], w["dg"], *w["ln"][1],
                            lambda_init)
        xT = _ffn_layer(xT, *w["ffn"][1], *w["ln"][2])
    return jnp.swapaxes(xT, 1, 2)


def kernel(x_prompt, x_sample, ln_g, ln_b, ffn_w_gate, ffn_w_up, ffn_w_down, w_in, mla_q_norm,
           mla_kv_norm, mla_w_uq, mla_w_ukv, diff_lambda, diff_subln, w_out):
    params = (ln_g, ln_b, ffn_w_gate, ffn_w_up, ffn_w_down, w_in, mla_q_norm, mla_kv_norm,
              mla_w_uq, mla_w_ukv, diff_lambda, diff_subln, w_out)
    layers = [_prep_layer(i, *params) for i in range(DEPTH)]
    return _run(x_prompt, layers), _run(x_sample, layers)
```

```python
import functools
import math

import jax
import jax.numpy as jnp
from jax import lax
from jax.experimental import pallas as pl
from jax.experimental.pallas import tpu as pltpu

F32 = jnp.float32
BF16 = jnp.bfloat16

D_MODEL = 1024
D_FF = 2816
DEPTH = 4
ROPE_THETA = 10000.0
NORM_EPS = 1e-5
A_HEADS, A_NOPE, A_ROPE, A_V = 8, 64, 32, 64
A_Q_LORA, A_KV_LORA = 384, 256
A_QK_PAD = 128
B_HEADS, B_D = 4, 32
B_V = 2 * B_D
C_HEADS, C_DIM = 4, 64
C_BRANCHES = ((128, 1), (512, 4), (2048, 16))
C_HALF = 64
IN_SPLITS = (A_Q_LORA, A_KV_LORA, A_ROPE, 256, 256, 256, 256, 256, 256)
IN_OFFS = tuple(int(sum(IN_SPLITS[:i])) for i in range(len(IN_SPLITS) + 1))
ALPHA = (2 * DEPTH) ** 0.25
LOG2E = math.log2(math.e)

VMEM_LIMIT_BYTES = 56 * 1024 * 1024
SEQ_TILE = 512
FF_CHUNKS = 2
ATTN_TQ = 1024
ATTN_TK = 512
ATTN_CHAINS = 4
ATTN_UNROLL = 2
PIPE_SLOTS = 2
SUM_ROWS = 16
DIL_QBLK = 128
DIL_KBLK = 256
DIL_SUPER = 2048
DIL_UNROLL = 4
NEG_BIG = -1e30


def _cparams(sem):
    return pltpu.CompilerParams(dimension_semantics=sem, vmem_limit_bytes=VMEM_LIMIT_BYTES)


def _layer_norm_rows(y, g, b):
    mu = jnp.mean(y, axis=0, keepdims=True)
    d = y - mu
    var = jnp.mean(d * d, axis=0, keepdims=True)
    return d * lax.rsqrt(var + NORM_EPS) * g + b


def _rms_norm_rows(y, g):
    ms = jnp.mean(y * y, axis=0, keepdims=True)
    return y * lax.rsqrt(ms + NORM_EPS) * g


def _rope_rows(x, c, s, groups, dim):
    half = dim // 2
    t = x.shape[-1]
    xg = x.reshape(groups, 2, half, t)
    x1, x2 = xg[:, 0], xg[:, 1]
    o1 = x1 * c - x2 * s
    o2 = x2 * c + x1 * s
    return jnp.stack([o1, o2], axis=1).reshape(groups * dim, t)


def _ffn_kernel(x_ref, wgu_ref, wd_ref, g_ref, b_ref, o_ref, xb_ref, acc_ref, y_ref, *, tf):
    t, j = pl.program_id(0), pl.program_id(1)

    def swiglu_chunk(xb):
        gu = jnp.dot(wgu_ref[0], xb, preferred_element_type=F32)
        gate, up = gu[:tf], gu[tf:]
        h = (gate * jax.nn.sigmoid(gate) * up).astype(BF16)
        return jnp.dot(wd_ref[0], h, preferred_element_type=F32)

    @pl.when((t == 0) & (j == 0))
    def _():
        y_ref[...] = jnp.zeros(y_ref.shape, F32)

    @pl.when(j == 0)
    def _():
        xb = x_ref[0].astype(BF16)
        xb_ref[...] = xb
        acc_ref[...] = swiglu_chunk(xb)
        o_ref[0] = _layer_norm_rows(y_ref[...], g_ref[...], b_ref[...])

    @pl.when(j == FF_CHUNKS - 1)
    def _():
        y_ref[...] = ALPHA * x_ref[0] + 0.5 * (acc_ref[...] + swiglu_chunk(xb_ref[...]))


def _ffn_layer(xT, wguT, wdT, g, b):
    bsz, _, seq = xT.shape
    ts = SEQ_TILE
    tf = D_FF // FF_CHUNKS
    nt = seq // ts
    tiles = bsz * nt

    def x_map(t, j):
        tt = jnp.minimum(t, tiles - 1)
        return (tt // nt, 0, tt % nt)

    def o_map(t, j):
        tt = jnp.maximum(t - 1, 0)
        return (tt // nt, 0, tt % nt)

    return pl.pallas_call(
        functools.partial(_ffn_kernel, tf=tf),
        grid=(tiles + 1, FF_CHUNKS),
        in_specs=[
            pl.BlockSpec((1, D_MODEL, ts), x_map),
            pl.BlockSpec((1, 2 * tf, D_MODEL), lambda t, j: (j, 0, 0)),
            pl.BlockSpec((1, D_MODEL, tf), lambda t, j: (j, 0, 0)),
            pl.BlockSpec((D_MODEL, 1), lambda t, j: (0, 0)),
            pl.BlockSpec((D_MODEL, 1), lambda t, j: (0, 0)),
        ],
        out_specs=pl.BlockSpec((1, D_MODEL, ts), o_map),
        out_shape=jax.ShapeDtypeStruct(xT.shape, F32),
        scratch_shapes=[pltpu.VMEM((D_MODEL, ts), BF16), pltpu.VMEM((D_MODEL, ts), F32),
                        pltpu.VMEM((D_MODEL, ts), F32)],
        compiler_params=_cparams(("arbitrary", "arbitrary")),
        name="ffn",
    )(xT, wguT, wdT, g, b)


def _inproj_kernel(x_ref, win_ref, gq_ref, gkv_ref, wuq_ref, wukv_ref,
                   c32_ref, s32_ref, c64_ref, s64_ref,
                   qa_ref, ka_ref, va_ref, qd_ref, kd_ref, vd_ref, qc_ref, kc_ref, vc_ref):
    xb = x_ref[0].astype(BF16)
    ts = xb.shape[-1]
    c32, s32 = c32_ref[...], s32_ref[...]
    c64, s64 = c64_ref[...], s64_ref[...]

    def proj(lo, hi):
        return jnp.dot(win_ref[lo:hi, :], xb, preferred_element_type=F32)

    nq = _rms_norm_rows(proj(IN_OFFS[0], IN_OFFS[1]), gq_ref[...]).astype(BF16)
    qa = jnp.dot(wuq_ref[...], nq, preferred_element_type=F32)
    nkv = _rms_norm_rows(proj(IN_OFFS[1], IN_OFFS[2]), gkv_ref[...]).astype(BF16)
    kva = jnp.dot(wukv_ref[...], nkv, preferred_element_type=F32)
    kpe = _rope_rows(proj(IN_OFFS[2], IN_OFFS[3]), c32, s32, 1, A_ROPE)
    qscale = (A_NOPE + A_ROPE) ** -0.5 * LOG2E
    zpad = jnp.zeros((A_QK_PAD - A_NOPE - A_ROPE, ts), F32)
    dqk = A_NOPE + A_ROPE
    for h in range(A_HEADS):
        qh = qa[h * dqk:(h + 1) * dqk]
        q_pe = _rope_rows(qh[A_NOPE:], c32, s32, 1, A_ROPE)
        qa_ref[0, h, 0] = (jnp.concatenate([qh[:A_NOPE], q_pe, zpad], axis=0) * qscale).astype(BF16)
        kvh = kva[h * (A_NOPE + A_V):(h + 1) * (A_NOPE + A_V)]
        kT = jnp.concatenate([kvh[:A_NOPE], kpe, zpad], axis=0)
        ka_ref[0, h] = kT.T.astype(BF16)
        va_ref[0, h, 0] = kvh[A_NOPE:].astype(BF16)

    pb = proj(IN_OFFS[3], IN_OFFS[6])
    qd = _rope_rows(pb[0:256], c32, s32, 2 * B_HEADS, B_D) * (B_D ** -0.5 * LOG2E)
    qd_ref[0, 0] = qd.astype(BF16)
    kd = _rope_rows(pb[256:512], c32, s32, 2 * B_HEADS, B_D)
    kd_ref[0] = kd.T.astype(BF16)
    for h in range(B_HEADS):
        vd_ref[0, h, 0] = pb[512 + h * B_V:512 + (h + 1) * B_V].astype(BF16)

    pc = proj(IN_OFFS[6], IN_OFFS[9])
    qc = _rope_rows(pc[0:256], c64, s64, C_HEADS, C_DIM) * (C_DIM ** -0.5 * LOG2E)
    kc = _rope_rows(pc[256:512], c64, s64, C_HEADS, C_DIM)
    qc_ref[0] = qc.T
    kc_ref[0] = kc.T
    vc_ref[0] = pc[512:768].T


def _inproj_layer(xT, winT, gq, gkv, wuqT, wukvT, rope):
    bsz, _, seq = xT.shape
    ts = SEQ_TILE
    nt = seq // ts
    c32, s32, c64, s64 = rope
    full = lambda shape: pl.BlockSpec(shape, lambda bi, i: (0,) * len(shape))
    tab = lambda rows: pl.BlockSpec((rows, ts), lambda bi, i: (0, i))
    out_shape = (
        jax.ShapeDtypeStruct((bsz, A_HEADS, nt, A_QK_PAD, ts), BF16),
        jax.ShapeDtypeStruct((bsz, A_HEADS, seq, A_QK_PAD), BF16),
        jax.ShapeDtypeStruct((bsz, A_HEADS, nt, A_V, ts), BF16),
        jax.ShapeDtypeStruct((bsz, nt, 2 * B_HEADS * B_D, ts), BF16),
        jax.ShapeDtypeStruct((bsz, seq, 2 * B_HEADS * B_D), BF16),
        jax.ShapeDtypeStruct((bsz, B_HEADS, nt, B_V, ts), BF16),
        jax.ShapeDtypeStruct((bsz, seq, C_HEADS * C_DIM), F32),
        jax.ShapeDtypeStruct((bsz, seq, C_HEADS * C_DIM), F32),
        jax.ShapeDtypeStruct((bsz, seq, C_HEADS * C_DIM), F32),
    )
    out_specs = (
        pl.BlockSpec((1, A_HEADS, 1, A_QK_PAD, ts), lambda bi, i: (bi, 0, i, 0, 0)),
        pl.BlockSpec((1, A_HEADS, ts, A_QK_PAD), lambda bi, i: (bi, 0, i, 0)),
        pl.BlockSpec((1, A_HEADS, 1, A_V, ts), lambda bi, i: (bi, 0, i, 0, 0)),
        pl.BlockSpec((1, 1, 256, ts), lambda bi, i: (bi, i, 0, 0)),
        pl.BlockSpec((1, ts, 256), lambda bi, i: (bi, i, 0)),
        pl.BlockSpec((1, B_HEADS, 1, B_V, ts), lambda bi, i: (bi, 0, i, 0, 0)),
        pl.BlockSpec((1, ts, 256), lambda bi, i: (bi, i, 0)),
        pl.BlockSpec((1, ts, 256), lambda bi, i: (bi, i, 0)),
        pl.BlockSpec((1, ts, 256), lambda bi, i: (bi, i, 0)),
    )
    return pl.pallas_call(
        _inproj_kernel,
        grid=(bsz, nt),
        in_specs=[
            pl.BlockSpec((1, D_MODEL, ts), lambda bi, i: (bi, 0, i)),
            full(winT.shape), full(gq.shape), full(gkv.shape), full(wuqT.shape), full(wukvT.shape),
            tab(A_ROPE // 2), tab(A_ROPE // 2), tab(C_DIM // 2), tab(C_DIM // 2),
        ],
        out_specs=out_specs,
        out_shape=out_shape,
        compiler_params=_cparams(("parallel", "parallel")),
        name="inproj",
    )(xT, winT, gq, gkv, wuqT, wukvT, c32, s32, c64, s64)


def _attn_kernel(q_ref, k_ref, v_ref, o_ref, s_ref, p_ref, *, tk, nk, group_rows):
    chains, tc = s_ref.shape[1], s_ref.shape[3]
    qv = q_ref.at[0] if len(q_ref.shape) == 4 else q_ref.at[0, 0]
    kv = k_ref.at[0] if len(k_ref.shape) == 3 else k_ref.at[0, 0]
    ts = qv.shape[-1]
    assert ts % tc == 0 and (chains * tc) % ts == 0

    def query_tile(qi, _):
        _attn_query_tile(qi, qv, kv, v_ref, o_ref, s_ref, p_ref, tk=tk, nk=nk,
                         group_rows=group_rows)
        return 0

    lax.fori_loop(0, (qv.shape[0] * ts) // (chains * tc), query_tile, 0)


def _attn_query_tile(qi, qv, kv, v_ref, o_ref, s_ref, p_ref, *, tk, nk, group_rows):
    chains, tc = s_ref.shape[1], s_ref.shape[3]
    ts = qv.shape[-1]
    qs = []
    for c in range(chains):
        q = qv[qi * (chains * tc // ts) + (c * tc) // ts, :, (c * tc) % ts:(c * tc) % ts + tc]
        if group_rows:
            rows = lax.broadcasted_iota(jnp.int32, q.shape, 0)
            lo = pl.program_id(1) * group_rows
            q = jnp.where((rows >= lo) & (rows < lo + group_rows), q, jnp.zeros_like(q))
        qs.append(q)

    dv = v_ref.shape[-2]
    ones_rows = (lax.broadcasted_iota(jnp.int32, (SUM_ROWS, tk), 0) == 0).astype(BF16)

    def score_matmul(i, slot):
        start = pl.multiple_of(i * tk, tk)
        kc = kv[pl.ds(start, tk), :]
        cmax = []
        for c in range(chains):
            s = jnp.dot(kc, qs[c], preferred_element_type=F32)
            s_ref[slot, c] = s
            cmax.append(jnp.max(s, axis=0, keepdims=True))
        return tuple(cmax)

    vsub = v_ref.shape[-1] // tk
    assert PIPE_SLOTS % vsub == 0

    def value_matmul(i, slot, carry):
        part = slot % vsub
        blk = i // vsub if isinstance(i, int) else lax.div(i, vsub)
        vc = jnp.concatenate([v_ref[0, 0, blk, :, part * tk:(part + 1) * tk], ones_rows],
                             axis=0)
        pvs = [jnp.dot(vc, p_ref[slot, c], preferred_element_type=F32) for c in range(chains)]
        return tuple((m, alpha * acc + pvs[c], alpha) for c, (m, acc, alpha) in enumerate(carry))

    def step(i, carry, cmax, slot, first=False, last=False):
        nxt, prv = (slot + 1) % PIPE_SLOTS, (slot - 1) % PIPE_SLOTS
        cmax_next = None if last else score_matmul(i + 1, nxt)
        if not first:
            carry = value_matmul(i - 1, prv, carry)
        out = []
        for c, (m, acc, _) in enumerate(carry):
            m_new = jnp.maximum(m, cmax[c])
            p_ref[slot, c] = jnp.exp2(s_ref[slot, c] - m_new).astype(BF16)
            out.append((m_new, acc, jnp.exp2(m - m_new)))
        return tuple(out), cmax_next

    assert ATTN_UNROLL % PIPE_SLOTS == 0
    lead = 2 + (nk - 4) % ATTN_UNROLL

    def steps(j, state):
        carry, cmax = state
        for u in range(ATTN_UNROLL):
            carry, cmax = step(ATTN_UNROLL * j + lead + u, carry, cmax, (lead + u) % PIPE_SLOTS)
        return carry, cmax

    carry = tuple((jnp.full((1, tc), NEG_BIG, F32), jnp.zeros((dv + SUM_ROWS, tc), F32),
                   jnp.ones((1, tc), F32)) for _ in range(chains))
    cmax = score_matmul(0, 0)
    for i in range(lead):
        carry, cmax = step(i, carry, cmax, i % PIPE_SLOTS, first=(i == 0))
    carry, cmax = lax.fori_loop(0, (nk - 2 - lead) // ATTN_UNROLL, steps, (carry, cmax))
    carry, cmax = step(nk - 2, carry, cmax, (nk - 2) % PIPE_SLOTS)
    carry, _ = step(nk - 1, carry, cmax, (nk - 1) % PIPE_SLOTS, last=True)
    carry = value_matmul(nk - 1, (nk - 1) % PIPE_SLOTS, carry)
    for c, (_, acc, _) in enumerate(carry):
        o_ref[0, 0, qi * chains + c] = acc[:dv] / acc[dv:dv + 1]


def _attention(qT, k, vT, *, heads, shared_qk, v_heads):
    bsz, nt, ts = qT.shape[0], qT.shape[-3], qT.shape[-1]
    seq = nt * ts
    nv, dv, tv = vT.shape[2], vT.shape[3], vT.shape[4]
    tk, tc = ATTN_TK, ATTN_TQ // ATTN_CHAINS
    nk = seq // tk
    vrep = heads // v_heads
    if shared_qk:
        dq_all = qT.shape[2]
        q_spec = pl.BlockSpec((1, nt, dq_all, ts), lambda bi, h: (bi, 0, 0, 0))
        k_spec = pl.BlockSpec((1, seq, dq_all), lambda bi, h: (bi, 0, 0))
        group_rows = dq_all // heads
    else:
        dq = qT.shape[3]
        q_spec = pl.BlockSpec((1, 1, nt, dq, ts), lambda bi, h: (bi, h, 0, 0, 0))
        k_spec = pl.BlockSpec((1, 1, seq, dq), lambda bi, h: (bi, h, 0, 0))
        group_rows = 0
    return pl.pallas_call(
        functools.partial(_attn_kernel, tk=tk, nk=nk, group_rows=group_rows),
        scratch_shapes=[pltpu.VMEM((PIPE_SLOTS, ATTN_CHAINS, tk, tc), F32),
                        pltpu.VMEM((PIPE_SLOTS, ATTN_CHAINS, tk, tc), BF16)],
        grid=(bsz, heads),
        in_specs=[
            q_spec, k_spec,
            pl.BlockSpec((1, 1, nv, dv, tv), lambda bi, h: (bi, h // vrep, 0, 0, 0)),
        ],
        out_specs=pl.BlockSpec((1, 1, seq // tc, dv, tc), lambda bi, h: (bi, h, 0, 0, 0)),
        out_shape=jax.ShapeDtypeStruct((bsz, heads, seq // tc, dv, tc), F32),
        compiler_params=_cparams(("parallel", "parallel")),
        name="attn_shared" if shared_qk else "attn",
    )(qT, k, vT)


def _dilated_kernel(q_ref, k_ref, v_ref, o_ref, m_ref, l_ref, acc_ref, *, seq, sup):
    sb = pl.program_id(2)
    p0 = sb * sup
    m_ref[...] = jnp.full(m_ref.shape, NEG_BIG, F32)
    l_ref[...] = jnp.zeros(l_ref.shape, F32)
    acc_ref[...] = jnp.zeros(acc_ref.shape, F32)
    pair = 2 * C_DIM
    lane = lax.broadcasted_iota(jnp.int32, (1, pair), 1)
    h0 = lane < C_DIM
    qi = lax.broadcasted_iota(jnp.int32, (DIL_QBLK, 1), 0)
    ki = lax.broadcasted_iota(jnp.int32, (1, DIL_KBLK), 1)
    ones0 = jnp.broadcast_to(h0.astype(F32), (DIL_KBLK, pair))
    ones1 = 1.0 - ones0

    for _, dil in C_BRANCHES:
        cls_len = seq // dil
        blocks = sup // (DIL_QBLK * dil)

        def body(tt, carry, dil=dil, cls_len=cls_len, blocks=blocks):
            blk = []
            for u in range(DIL_UNROLL):
                t = tt * DIL_UNROLL + u
                r = lax.div(t, blocks)
                jb = lax.rem(t, blocks)
                c0 = sb * (sup // dil) + DIL_QBLK * jb
                ks = jnp.clip(c0 - C_HALF, 0, cls_len - DIL_KBLK)
                qrow = r + dil * DIL_QBLK * jb
                ssl = pl.ds(qrow, DIL_QBLK, stride=dil)
                ksl = pl.ds(r + dil * ks, DIL_KBLK, stride=dil)
                kf = k_ref[0, ksl, :]
                vf = v_ref[0, ksl, :]
                kcat = jnp.concatenate([jnp.where(h0, kf, 0.0), jnp.where(h0, 0.0, kf)], axis=0)
                vaug = jnp.concatenate(
                    [jnp.concatenate([jnp.where(h0, vf, 0.0), ones0], axis=1),
                     jnp.concatenate([jnp.where(h0, 0.0, vf), ones1], axis=1)], axis=0)
                blk.append(dict(
                    ssl=ssl,
                    q=q_ref[0, pl.ds(p0 + qrow, DIL_QBLK, stride=dil), :].astype(BF16),
                    kcat=kcat.astype(BF16),
                    vaug=vaug.astype(BF16),
                    band=jnp.abs((c0 + qi) - (ks + ki)) <= C_HALF,
                    acc=acc_ref[ssl, :], m=m_ref[ssl, :], l=l_ref[ssl, :]))
            for b in blk:
                b["s"] = lax.dot_general(b["q"], b["kcat"], (((1,), (1,)), ((), ())),
                                         preferred_element_type=F32)
            for b in blk:
                s0 = jnp.where(b["band"], b["s"][:, :DIL_KBLK], NEG_BIG)
                s1 = jnp.where(b["band"], b["s"][:, DIL_KBLK:], NEG_BIG)
                t0 = jnp.maximum(s0[:, :pair], s0[:, pair:])
                t1 = jnp.maximum(s1[:, :pair], s1[:, pair:])
                m0 = jnp.max(jnp.where(h0, jnp.maximum(t0, b["m"]), t0), axis=1, keepdims=True)
                m1 = jnp.max(jnp.where(h0, t1, jnp.maximum(t1, b["m"])), axis=1, keepdims=True)
                m_new = jnp.where(h0, m0, m1)
                b["p"] = jnp.concatenate([jnp.exp2(s0 - m0), jnp.exp2(s1 - m1)],
                                         axis=1).astype(BF16)
                b["alpha"] = jnp.exp2(b["m"] - m_new)
                b["m"] = m_new
            for b in blk:
                pv = jnp.dot(b["p"], b["vaug"], preferred_element_type=F32)
                b["acc"] = b["alpha"] * b["acc"] + pv[:, :pair]
                b["l"] = b["alpha"] * b["l"] + pv[:, pair:]
            for b in blk:
                acc_ref[b["ssl"], :] = b["acc"]
                m_ref[b["ssl"], :] = b["m"]
                l_ref[b["ssl"], :] = b["l"]
            return carry

        lax.fori_loop(0, dil * blocks // DIL_UNROLL, body, 0)

    o_ref[0] = acc_ref[...] / l_ref[...]


def _dilated_attention(qc, kc, vc):
    bsz, seq, width = qc.shape
    sup = min(DIL_SUPER, seq)
    pair = 2 * C_DIM
    in_spec = pl.BlockSpec((1, seq, pair), lambda bi, hp, sb: (bi, 0, hp))
    return pl.pallas_call(
        functools.partial(_dilated_kernel, seq=seq, sup=sup),
        grid=(bsz, width // pair, seq // sup),
        in_specs=[in_spec, in_spec, in_spec],
        out_specs=pl.BlockSpec((1, sup, pair), lambda bi, hp, sb: (bi, sb, hp)),
        out_shape=jax.ShapeDtypeStruct((bsz, seq, width), F32),
        scratch_shapes=[pltpu.VMEM((sup, pair), F32)] * 3,
        compiler_params=_cparams(("parallel", "parallel", "arbitrary")),
        name="dilated",
    )(qc, kc, vc)


def _outproj_kernel(x_ref, oa_ref, od_ref, oc_ref, woa_ref, wob_ref, woc_ref, lam_ref, dg_ref,
                    g_ref, b_ref, o_ref, *, lambda_init):
    ts = x_ref.shape[-1]
    lf = lam_ref[...]
    lam = (jnp.exp(jnp.sum(lf[0:1] * lf[1:2], axis=1, keepdims=True))
           - jnp.exp(jnp.sum(lf[2:3] * lf[3:4], axis=1, keepdims=True)) + lambda_init)
    def tile(ref):
        return jnp.concatenate([ref[0, :, c] for c in range(ref.shape[2])], axis=-1)

    od = tile(od_ref).reshape(B_HEADS, 2, B_V, ts)
    ob = od[:, 0] - lam * od[:, 1]
    ms = jnp.mean(ob * ob, axis=1, keepdims=True)
    ob = ob * lax.rsqrt(ms + NORM_EPS) * dg_ref[...] * (1.0 - lambda_init)
    ob = ob.reshape(B_HEADS * B_V, ts).astype(BF16)
    oa = tile(oa_ref).reshape(A_HEADS * A_V, ts).astype(BF16)
    y = jnp.dot(woa_ref[...], oa, preferred_element_type=F32)
    y += jnp.dot(wob_ref[...], ob, preferred_element_type=F32)
    y += lax.dot_general(woc_ref[...], oc_ref[0].astype(BF16), (((1,), (1,)), ((), ())),
                         preferred_element_type=F32)
    o_ref[0] = _layer_norm_rows(ALPHA * x_ref[0] + y, g_ref[...], b_ref[...])


def _outproj_layer(xT, oaT, odT, oc, woaT, wobT, wocT, lam, dg, g, b, lambda_init):
    bsz, _, seq = xT.shape
    ts = SEQ_TILE
    full = lambda a: pl.BlockSpec(a.shape, lambda bi, i: (0,) * a.ndim)
    col = lambda rows: pl.BlockSpec((1, rows, ts), lambda bi, i: (bi, 0, i))
    heads, _, dv, tc = oaT.shape[1:]
    chunks = pl.BlockSpec((1, heads, ts // tc, dv, tc), lambda bi, i: (bi, 0, i, 0, 0))
    return pl.pallas_call(
        functools.partial(_outproj_kernel, lambda_init=lambda_init),
        grid=(bsz, seq // ts),
        in_specs=[
            col(D_MODEL), chunks, chunks,
            pl.BlockSpec((1, ts, C_HEADS * C_DIM), lambda bi, i: (bi, i, 0)),
            full(woaT), full(wobT), full(wocT), full(lam), full(dg), full(g), full(b),
        ],
        out_specs=col(D_MODEL),
        out_shape=jax.ShapeDtypeStruct(xT.shape, F32),
        compiler_params=_cparams(("parallel", "parallel")),
        name="outproj",
    )(xT, oaT, odT, oc, woaT, wobT, wocT, lam, dg, g, b)


def _rope_tables(seq, dim):
    inv = 1.0 / (ROPE_THETA ** (jnp.arange(0, dim, 2, dtype=F32) / dim))
    ang = inv[:, None] * jnp.arange(seq, dtype=F32)[None, :]
    return jnp.cos(ang), jnp.sin(ang)


def _prep_layer(i, ln_g, ln_b, ffn_w_gate, ffn_w_up, ffn_w_down, w_in, mla_q_norm, mla_kv_norm,
                mla_w_uq, mla_w_ukv, diff_lambda, diff_subln, w_out):
    tf = D_FF // FF_CHUNKS
    col = lambda v: v.astype(F32)[:, None]
    ffn = []
    for s in range(2):
        wg = ffn_w_gate[i, s].T.reshape(FF_CHUNKS, tf, D_MODEL)
        wu = ffn_w_up[i, s].T.reshape(FF_CHUNKS, tf, D_MODEL)
        wgu = jnp.concatenate([wg, wu], axis=1).astype(BF16)
        wd = ffn_w_down[i, s].T.reshape(D_MODEL, FF_CHUNKS, tf).transpose(1, 0, 2).astype(BF16)
        ffn.append((wgu, wd))
    woT = w_out[i].T.astype(BF16)
    na, nb = A_HEADS * A_V, B_HEADS * B_V
    return dict(
        ffn=ffn,
        ln=[(col(ln_g[i, s]), col(ln_b[i, s])) for s in range(3)],
        winT=w_in[i].T.astype(BF16),
        gq=col(mla_q_norm[i]), gkv=col(mla_kv_norm[i]),
        wuqT=mla_w_uq[i].T.astype(BF16), wukvT=mla_w_ukv[i].T.astype(BF16),
        woaT=woT[:, :na], wobT=woT[:, na:na + nb], wocT=woT[:, na + nb:],
        lam=diff_lambda[i].astype(F32), dg=col(diff_subln[i]),
    )


def _run(x, layers):
    bsz, seq, _ = x.shape
    assert seq % DIL_SUPER == 0 and seq // C_BRANCHES[-1][1] >= DIL_KBLK, seq
    assert seq % SEQ_TILE == 0 and seq % ATTN_TQ == 0 and FF_CHUNKS == 2
    rope = _rope_tables(seq, A_ROPE) + _rope_tables(seq, C_DIM)
    xT = jnp.swapaxes(x, 1, 2)
    for i, w in enumerate(layers):
        lambda_init = 0.8 - 0.6 * math.exp(-0.3 * i)
        xT = _ffn_layer(xT, *w["ffn"][0], *w["ln"][0])
        qaT, ka, vaT, qdT, kd, vdT, qc, kc, vc = _inproj_layer(
            xT, w["winT"], w["gq"], w["gkv"], w["wuqT"], w["wukvT"], rope)
        oaT = _attention(qaT, ka, vaT, heads=A_HEADS, shared_qk=False, v_heads=A_HEADS)
        odT = _attention(qdT, kd, vdT, heads=2 * B_HEADS, shared_qk=True, v_heads=B_HEADS)
        oc = _dilated_attention(qc, kc, vc)
        xT = _outproj_layer(xT, oaT, odT, oc, w["woaT"], w["wobT"], w["wocT"], w["lam"], w["dg"],
                            *w["ln"][1], lambda_init)
        xT = _ffn_layer(xT, *w["ffn"][1], *w["ln"][2])
    return jnp.swapaxes(xT, 1, 2)


def kernel(x_prompt, x_sample, ln_g, ln_b, ffn_w_gate, ffn_w_up, ffn_w_down, w_in, mla_q_norm,
           mla_kv_norm, mla_w_uq, mla_w_ukv, diff_lambda, diff_subln, w_out):
    params = (ln_g, ln_b, ffn_w_gate, ffn_w_up, ffn_w_down, w_in, mla_q_norm, mla_kv_norm,
              mla_w_uq, mla_w_ukv, diff_lambda, diff_subln, w_out)
    layers = [_prep_layer(i, *params) for i in range(DEPTH)]
    return _run(x_prompt, layers), _run(x_sample, layers)
```

```python
import functools
import math

import jax
import jax.numpy as jnp
from jax import lax
from jax.experimental import pallas as pl
from jax.experimental.pallas import tpu as pltpu

F32 = jnp.float32
BF16 = jnp.bfloat16

D_MODEL = 1024
D_FF = 2816
DEPTH = 4
ROPE_THETA = 10000.0
NORM_EPS = 1e-5
A_HEADS, A_NOPE, A_ROPE, A_V = 8, 64, 32, 64
A_Q_LORA, A_KV_LORA = 384, 256
A_QK_PAD = 128
B_HEADS, B_D = 4, 32
B_V = 2 * B_D
C_HEADS, C_DIM = 4, 64
C_BRANCHES = ((128, 1), (512, 4), (2048, 16))
C_HALF = 64
IN_SPLITS = (A_Q_LORA, A_KV_LORA, A_ROPE, 256, 256, 256, 256, 256, 256)
IN_OFFS = tuple(int(sum(IN_SPLITS[:i])) for i in range(len(IN_SPLITS) + 1))
ALPHA = (2 * DEPTH) ** 0.25
LOG2E = math.log2(math.e)

VMEM_LIMIT_BYTES = 56 * 1024 * 1024
SEQ_TILE = 512
FF_CHUNKS = 2
ATTN_TQ = 1024
ATTN_TK = 512
ATTN_CHAINS = 4
ATTN_UNROLL = 4
PIPE_SLOTS = 2
SUM_ROWS = 16
DIL_QBLK = 128
DIL_KBLK = 256
DIL_SUPER = 2048
DIL_UNROLL = 4
NEG_BIG = -1e30
MAX_SHIFT_LAG = 64.0


def _cparams(sem):
    return pltpu.CompilerParams(dimension_semantics=sem, vmem_limit_bytes=VMEM_LIMIT_BYTES)


def _layer_norm_rows(y, g, b):
    mu = jnp.mean(y, axis=0, keepdims=True)
    d = y - mu
    var = jnp.mean(d * d, axis=0, keepdims=True)
    return d * lax.rsqrt(var + NORM_EPS) * g + b


def _rms_norm_rows(y, g):
    ms = jnp.mean(y * y, axis=0, keepdims=True)
    return y * lax.rsqrt(ms + NORM_EPS) * g


def _rope_rows(x, c, s, groups, dim):
    half = dim // 2
    t = x.shape[-1]
    xg = x.reshape(groups, 2, half, t)
    x1, x2 = xg[:, 0], xg[:, 1]
    o1 = x1 * c - x2 * s
    o2 = x2 * c + x1 * s
    return jnp.stack([o1, o2], axis=1).reshape(groups * dim, t)


def _ffn_kernel(x_ref, wgu_ref, wd_ref, g_ref, b_ref, o_ref, xb_ref, acc_ref, y_ref, *, tf):
    t, j = pl.program_id(0), pl.program_id(1)

    def swiglu_chunk(xb):
        gu = jnp.dot(wgu_ref[0], xb, preferred_element_type=F32)
        gate, up = gu[:tf], gu[tf:]
        h = (gate * jax.nn.sigmoid(gate) * up).astype(BF16)
        return jnp.dot(wd_ref[0], h, preferred_element_type=F32)

    @pl.when((t == 0) & (j == 0))
    def _():
        y_ref[...] = jnp.zeros(y_ref.shape, F32)

    @pl.when(j == 0)
    def _():
        xb = x_ref[0].astype(BF16)
        xb_ref[...] = xb
        acc_ref[...] = swiglu_chunk(xb)
        o_ref[0] = _layer_norm_rows(y_ref[...], g_ref[...], b_ref[...])

    @pl.when(j == FF_CHUNKS - 1)
    def _():
        y_ref[...] = ALPHA * x_ref[0] + 0.5 * (acc_ref[...] + swiglu_chunk(xb_ref[...]))


def _ffn_layer(xT, wguT, wdT, g, b):
    bsz, _, seq = xT.shape
    ts = SEQ_TILE
    tf = D_FF // FF_CHUNKS
    nt = seq // ts
    tiles = bsz * nt

    def x_map(t, j):
        tt = jnp.minimum(t, tiles - 1)
        return (tt // nt, 0, tt % nt)

    def o_map(t, j):
        tt = jnp.maximum(t - 1, 0)
        return (tt // nt, 0, tt % nt)

    return pl.pallas_call(
        functools.partial(_ffn_kernel, tf=tf),
        grid=(tiles + 1, FF_CHUNKS),
        in_specs=[
            pl.BlockSpec((1, D_MODEL, ts), x_map),
            pl.BlockSpec((1, 2 * tf, D_MODEL), lambda t, j: (j, 0, 0)),
            pl.BlockSpec((1, D_MODEL, tf), lambda t, j: (j, 0, 0)),
            pl.BlockSpec((D_MODEL, 1), lambda t, j: (0, 0)),
            pl.BlockSpec((D_MODEL, 1), lambda t, j: (0, 0)),
        ],
        out_specs=pl.BlockSpec((1, D_MODEL, ts), o_map),
        out_shape=jax.ShapeDtypeStruct(xT.shape, F32),
        scratch_shapes=[pltpu.VMEM((D_MODEL, ts), BF16), pltpu.VMEM((D_MODEL, ts), F32),
                        pltpu.VMEM((D_MODEL, ts), F32)],
        compiler_params=_cparams(("arbitrary", "arbitrary")),
        name="ffn",
    )(xT, wguT, wdT, g, b)


def _inproj_kernel(x_ref, win_ref, gq_ref, gkv_ref, wuq_ref, wukv_ref,
                   c32_ref, s32_ref, c64_ref, s64_ref,
                   qa_ref, ka_ref, va_ref, qd_ref, kd_ref, vd_ref, qc_ref, kc_ref, vc_ref):
    xb = x_ref[0].astype(BF16)
    ts = xb.shape[-1]
    c32, s32 = c32_ref[...], s32_ref[...]
    c64, s64 = c64_ref[...], s64_ref[...]

    def proj(lo, hi):
        return jnp.dot(win_ref[lo:hi, :], xb, preferred_element_type=F32)

    nq = _rms_norm_rows(proj(IN_OFFS[0], IN_OFFS[1]), gq_ref[...]).astype(BF16)
    qa = jnp.dot(wuq_ref[...], nq, preferred_element_type=F32)
    nkv = _rms_norm_rows(proj(IN_OFFS[1], IN_OFFS[2]), gkv_ref[...]).astype(BF16)
    kva = jnp.dot(wukv_ref[...], nkv, preferred_element_type=F32)
    kpe = _rope_rows(proj(IN_OFFS[2], IN_OFFS[3]), c32, s32, 1, A_ROPE)
    qscale = (A_NOPE + A_ROPE) ** -0.5 * LOG2E
    zpad = jnp.zeros((A_QK_PAD - A_NOPE - A_ROPE, ts), F32)
    dqk = A_NOPE + A_ROPE
    for h in range(A_HEADS):
        qh = qa[h * dqk:(h + 1) * dqk]
        q_pe = _rope_rows(qh[A_NOPE:], c32, s32, 1, A_ROPE)
        qa_ref[0, h, 0] = (jnp.concatenate([qh[:A_NOPE], q_pe, zpad], axis=0) * qscale).astype(BF16)
        kvh = kva[h * (A_NOPE + A_V):(h + 1) * (A_NOPE + A_V)]
        kT = jnp.concatenate([kvh[:A_NOPE], kpe, zpad], axis=0)
        ka_ref[0, h] = kT.T.astype(BF16)
        va_ref[0, h, 0] = kvh[A_NOPE:].astype(BF16)

    pb = proj(IN_OFFS[3], IN_OFFS[6])
    qd = _rope_rows(pb[0:256], c32, s32, 2 * B_HEADS, B_D) * (B_D ** -0.5 * LOG2E)
    qd_ref[0, 0] = qd.astype(BF16)
    kd = _rope_rows(pb[256:512], c32, s32, 2 * B_HEADS, B_D)
    kd_ref[0] = kd.T.astype(BF16)
    for h in range(B_HEADS):
        vd_ref[0, h, 0] = pb[512 + h * B_V:512 + (h + 1) * B_V].astype(BF16)

    pc = proj(IN_OFFS[6], IN_OFFS[9])
    qc = _rope_rows(pc[0:256], c64, s64, C_HEADS, C_DIM) * (C_DIM ** -0.5 * LOG2E)
    kc = _rope_rows(pc[256:512], c64, s64, C_HEADS, C_DIM)
    qc_ref[0] = qc.T
    kc_ref[0] = kc.T
    vc_ref[0] = pc[512:768].T


def _inproj_layer(xT, winT, gq, gkv, wuqT, wukvT, rope):
    bsz, _, seq = xT.shape
    ts = SEQ_TILE
    nt = seq // ts
    c32, s32, c64, s64 = rope
    full = lambda shape: pl.BlockSpec(shape, lambda bi, i: (0,) * len(shape))
    tab = lambda rows: pl.BlockSpec((rows, ts), lambda bi, i: (0, i))
    out_shape = (
        jax.ShapeDtypeStruct((bsz, A_HEADS, nt, A_QK_PAD, ts), BF16),
        jax.ShapeDtypeStruct((bsz, A_HEADS, seq, A_QK_PAD), BF16),
        jax.ShapeDtypeStruct((bsz, A_HEADS, nt, A_V, ts), BF16),
        jax.ShapeDtypeStruct((bsz, nt, 2 * B_HEADS * B_D, ts), BF16),
        jax.ShapeDtypeStruct((bsz, seq, 2 * B_HEADS * B_D), BF16),
        jax.ShapeDtypeStruct((bsz, B_HEADS, nt, B_V, ts), BF16),
        jax.ShapeDtypeStruct((bsz, seq, C_HEADS * C_DIM), F32),
        jax.ShapeDtypeStruct((bsz, seq, C_HEADS * C_DIM), F32),
        jax.ShapeDtypeStruct((bsz, seq, C_HEADS * C_DIM), F32),
    )
    out_specs = (
        pl.BlockSpec((1, A_HEADS, 1, A_QK_PAD, ts), lambda bi, i: (bi, 0, i, 0, 0)),
        pl.BlockSpec((1, A_HEADS, ts, A_QK_PAD), lambda bi, i: (bi, 0, i, 0)),
        pl.BlockSpec((1, A_HEADS, 1, A_V, ts), lambda bi, i: (bi, 0, i, 0, 0)),
        pl.BlockSpec((1, 1, 256, ts), lambda bi, i: (bi, i, 0, 0)),
        pl.BlockSpec((1, ts, 256), lambda bi, i: (bi, i, 0)),
        pl.BlockSpec((1, B_HEADS, 1, B_V, ts), lambda bi, i: (bi, 0, i, 0, 0)),
        pl.BlockSpec((1, ts, 256), lambda bi, i: (bi, i, 0)),
        pl.BlockSpec((1, ts, 256), lambda bi, i: (bi, i, 0)),
        pl.BlockSpec((1, ts, 256), lambda bi, i: (bi, i, 0)),
    )
    return pl.pallas_call(
        _inproj_kernel,
        grid=(bsz, nt),
        in_specs=[
            pl.BlockSpec((1, D_MODEL, ts), lambda bi, i: (bi, 0, i)),
            full(winT.shape), full(gq.shape), full(gkv.shape), full(wuqT.shape), full(wukvT.shape),
            tab(A_ROPE // 2), tab(A_ROPE // 2), tab(C_DIM // 2), tab(C_DIM // 2),
        ],
        out_specs=out_specs,
        out_shape=out_shape,
        compiler_params=_cparams(("parallel", "parallel")),
        name="inproj",
    )(xT, winT, gq, gkv, wuqT, wukvT, c32, s32, c64, s64)


def _attn_kernel(q_ref, k_ref, v_ref, o_ref, s_ref, p_ref, *, tk, nk, group_rows):
    chains, tc = s_ref.shape[1], s_ref.shape[3]
    qv = q_ref.at[0] if len(q_ref.shape) == 4 else q_ref.at[0, 0]
    kv = k_ref.at[0] if len(k_ref.shape) == 3 else k_ref.at[0, 0]
    ts = qv.shape[-1]
    assert ts % tc == 0 and (chains * tc) % ts == 0

    def query_tile(qi, _):
        args = (qi, qv, kv, v_ref, o_ref, s_ref, p_ref)
        jump = _attn_query_tile_fast(*args, tk=tk, nk=nk, group_rows=group_rows)

        @pl.when(jump > MAX_SHIFT_LAG)
        def _():
            _attn_query_tile(*args, tk=tk, nk=nk, group_rows=group_rows)
        return 0

    lax.fori_loop(0, (qv.shape[0] * ts) // (chains * tc), query_tile, 0)


def _load_queries(qi, qv, chains, tc, group_rows):
    ts = qv.shape[-1]
    qs = []
    for c in range(chains):
        q = qv[qi * (chains * tc // ts) + (c * tc) // ts, :, (c * tc) % ts:(c * tc) % ts + tc]
        if group_rows:
            rows = lax.broadcasted_iota(jnp.int32, q.shape, 0)
            lo = pl.program_id(1) * group_rows
            q = jnp.where((rows >= lo) & (rows < lo + group_rows), q, jnp.zeros_like(q))
        qs.append(q)
    return qs


def _attn_query_tile_fast(qi, qv, kv, v_ref, o_ref, s_ref, p_ref, *, tk, nk, group_rows):
    chains, tc = s_ref.shape[1], s_ref.shape[3]
    qs = _load_queries(qi, qv, chains, tc, group_rows)
    dv = v_ref.shape[-2]
    ones_rows = (lax.broadcasted_iota(jnp.int32, (SUM_ROWS, tk), 0) == 0).astype(BF16)
    vsub = v_ref.shape[-1] // tk

    def scores(i):
        kc = kv[pl.ds(pl.multiple_of(i * tk, tk), tk), :]
        return [jnp.dot(kc, qs[c], preferred_element_type=F32) for c in range(chains)]

    def value_matmul(i, slot):
        blk = i // vsub if isinstance(i, int) else lax.div(i, vsub)
        part = slot % vsub
        vc = jnp.concatenate([v_ref[0, 0, blk, :, part * tk:(part + 1) * tk], ones_rows], axis=0)
        return [jnp.dot(vc, p_ref[slot, c], preferred_element_type=F32) for c in range(chains)]

    def step(i, carry, slot):
        ss = scores(i)
        pvs = value_matmul(i - 1, 1 - slot)
        out = []
        for c, (shift, acc, a1, a2, lag) in enumerate(carry):
            cmax = jnp.max(ss[c], axis=0, keepdims=True)
            p_ref[slot, c] = jnp.exp2(ss[c] - shift).astype(BF16)
            new_shift = jnp.maximum(shift, cmax)
            out.append((new_shift, a1 * acc + pvs[c], a2, jnp.exp2(shift - new_shift),
                        jnp.maximum(lag, cmax - shift)))
        return tuple(out)

    carry = []
    for c, s in enumerate(scores(0)):
        shift = jnp.max(s, axis=0, keepdims=True)
        p_ref[0, c] = jnp.exp2(s - shift).astype(BF16)
        one = jnp.ones((1, tc), F32)
        carry.append((shift, jnp.zeros((dv + SUM_ROWS, tc), F32), one, one, jnp.zeros((1, tc), F32)))
    assert ATTN_UNROLL % PIPE_SLOTS == 0 and PIPE_SLOTS == 2 and nk % 2 == 0
    lead = 2 + (nk - 2) % ATTN_UNROLL
    carry = tuple(carry)
    for i in range(1, lead):
        carry = step(i, carry, i % PIPE_SLOTS)

    def steps(j, carry):
        for u in range(ATTN_UNROLL):
            carry = step(ATTN_UNROLL * j + lead + u, carry, u % PIPE_SLOTS)
        return carry

    carry = lax.fori_loop(0, (nk - lead) // ATTN_UNROLL, steps, carry)
    pvs = value_matmul(nk - 1, (nk - 1) % PIPE_SLOTS)
    jump = jnp.zeros((1, tc), F32)
    for c, (_, acc, a1, _, lag) in enumerate(carry):
        acc = a1 * acc + pvs[c]
        o_ref[0, 0, qi * chains + c] = acc[:dv] / acc[dv:dv + 1]
        jump = jnp.maximum(jump, lag)
    return jnp.max(jump)


def _attn_query_tile(qi, qv, kv, v_ref, o_ref, s_ref, p_ref, *, tk, nk, group_rows):
    chains, tc = s_ref.shape[1], s_ref.shape[3]
    qs = _load_queries(qi, qv, chains, tc, group_rows)

    dv = v_ref.shape[-2]
    ones_rows = (lax.broadcasted_iota(jnp.int32, (SUM_ROWS, tk), 0) == 0).astype(BF16)

    def score_matmul(i, slot):
        start = pl.multiple_of(i * tk, tk)
        kc = kv[pl.ds(start, tk), :]
        cmax = []
        for c in range(chains):
            s = jnp.dot(kc, qs[c], preferred_element_type=F32)
            s_ref[slot, c] = s
            cmax.append(jnp.max(s, axis=0, keepdims=True))
        return tuple(cmax)

    vsub = v_ref.shape[-1] // tk
    assert PIPE_SLOTS % vsub == 0

    def value_matmul(i, slot, carry):
        part = slot % vsub
        blk = i // vsub if isinstance(i, int) else lax.div(i, vsub)
        vc = jnp.concatenate([v_ref[0, 0, blk, :, part * tk:(part + 1) * tk], ones_rows],
                             axis=0)
        pvs = [jnp.dot(vc, p_ref[slot, c], preferred_element_type=F32) for c in range(chains)]
        return tuple((m, alpha * acc + pvs[c], alpha) for c, (m, acc, alpha) in enumerate(carry))

    def step(i, carry, cmax, slot, first=False, last=False):
        nxt, prv = (slot + 1) % PIPE_SLOTS, (slot - 1) % PIPE_SLOTS
        cmax_next = None if last else score_matmul(i + 1, nxt)
        if not first:
            carry = value_matmul(i - 1, prv, carry)
        out = []
        for c, (m, acc, _) in enumerate(carry):
            m_new = jnp.maximum(m, cmax[c])
            p_ref[slot, c] = jnp.exp2(s_ref[slot, c] - m_new).astype(BF16)
            out.append((m_new, acc, jnp.exp2(m - m_new)))
        return tuple(out), cmax_next

    assert ATTN_UNROLL % PIPE_SLOTS == 0
    lead = 2 + (nk - 4) % ATTN_UNROLL

    def steps(j, state):
        carry, cmax = state
        for u in range(ATTN_UNROLL):
            carry, cmax = step(ATTN_UNROLL * j + lead + u, carry, cmax, (lead + u) % PIPE_SLOTS)
        return carry, cmax

    carry = tuple((jnp.full((1, tc), NEG_BIG, F32), jnp.zeros((dv + SUM_ROWS, tc), F32),
                   jnp.ones((1, tc), F32)) for _ in range(chains))
    cmax = score_matmul(0, 0)
    for i in range(lead):
        carry, cmax = step(i, carry, cmax, i % PIPE_SLOTS, first=(i == 0))
    carry, cmax = lax.fori_loop(0, (nk - 2 - lead) // ATTN_UNROLL, steps, (carry, cmax))
    carry, cmax = step(nk - 2, carry, cmax, (nk - 2) % PIPE_SLOTS)
    carry, _ = step(nk - 1, carry, cmax, (nk - 1) % PIPE_SLOTS, last=True)
    carry = value_matmul(nk - 1, (nk - 1) % PIPE_SLOTS, carry)
    for c, (_, acc, _) in enumerate(carry):
        o_ref[0, 0, qi * chains + c] = acc[:dv] / acc[dv:dv + 1]


def _attention(qT, k, vT, *, heads, shared_qk, v_heads):
    bsz, nt, ts = qT.shape[0], qT.shape[-3], qT.shape[-1]
    seq = nt * ts
    nv, dv, tv = vT.shape[2], vT.shape[3], vT.shape[4]
    tk, tc = ATTN_TK, ATTN_TQ // ATTN_CHAINS
    nk = seq // tk
    vrep = heads // v_heads
    if shared_qk:
        dq_all = qT.shape[2]
        q_spec = pl.BlockSpec((1, nt, dq_all, ts), lambda bi, h: (bi, 0, 0, 0))
        k_spec = pl.BlockSpec((1, seq, dq_all), lambda bi, h: (bi, 0, 0))
        group_rows = dq_all // heads
    else:
        dq = qT.shape[3]
        q_spec = pl.BlockSpec((1, 1, nt, dq, ts), lambda bi, h: (bi, h, 0, 0, 0))
        k_spec = pl.BlockSpec((1, 1, seq, dq), lambda bi, h: (bi, h, 0, 0))
        group_rows = 0
    return pl.pallas_call(
        functools.partial(_attn_kernel, tk=tk, nk=nk, group_rows=group_rows),
        scratch_shapes=[pltpu.VMEM((PIPE_SLOTS, ATTN_CHAINS, tk, tc), F32),
                        pltpu.VMEM((PIPE_SLOTS, ATTN_CHAINS, tk, tc), BF16)],
        grid=(bsz, heads),
        in_specs=[
            q_spec, k_spec,
            pl.BlockSpec((1, 1, nv, dv, tv), lambda bi, h: (bi, h // vrep, 0, 0, 0)),
        ],
        out_specs=pl.BlockSpec((1, 1, seq // tc, dv, tc), lambda bi, h: (bi, h, 0, 0, 0)),
        out_shape=jax.ShapeDtypeStruct((bsz, heads, seq // tc, dv, tc), F32),
        compiler_params=_cparams(("parallel", "parallel")),
        name="attn_shared" if shared_qk else "attn",
    )(qT, k, vT)


def _dilated_kernel(q_ref, k_ref, v_ref, o_ref, m_ref, l_ref, acc_ref, *, seq, sup):
    sb = pl.program_id(2)
    p0 = sb * sup
    m_ref[...] = jnp.full(m_ref.shape, NEG_BIG, F32)
    l_ref[...] = jnp.zeros(l_ref.shape, F32)
    acc_ref[...] = jnp.zeros(acc_ref.shape, F32)
    pair = 2 * C_DIM
    lane = lax.broadcasted_iota(jnp.int32, (1, pair), 1)
    h0 = lane < C_DIM
    qi = lax.broadcasted_iota(jnp.int32, (DIL_QBLK, 1), 0)
    ki = lax.broadcasted_iota(jnp.int32, (1, DIL_KBLK), 1)
    ones0 = jnp.broadcast_to(h0.astype(F32), (DIL_KBLK, pair))
    ones1 = 1.0 - ones0

    for _, dil in C_BRANCHES:
        cls_len = seq // dil
        blocks = sup // (DIL_QBLK * dil)

        def body(tt, carry, dil=dil, cls_len=cls_len, blocks=blocks):
            blk = []
            for u in range(DIL_UNROLL):
                t = tt * DIL_UNROLL + u
                r = lax.div(t, blocks)
                jb = lax.rem(t, blocks)
                c0 = sb * (sup // dil) + DIL_QBLK * jb
                ks = jnp.clip(c0 - C_HALF, 0, cls_len - DIL_KBLK)
                qrow = r + dil * DIL_QBLK * jb
                ssl = pl.ds(qrow, DIL_QBLK, stride=dil)
                ksl = pl.ds(r + dil * ks, DIL_KBLK, stride=dil)
                kf = k_ref[0, ksl, :]
                vf = v_ref[0, ksl, :]
                kcat = jnp.concatenate([jnp.where(h0, kf, 0.0), jnp.where(h0, 0.0, kf)], axis=0)
                vaug = jnp.concatenate(
                    [jnp.concatenate([jnp.where(h0, vf, 0.0), ones0], axis=1),
                     jnp.concatenate([jnp.where(h0, 0.0, vf), ones1], axis=1)], axis=0)
                blk.append(dict(
                    ssl=ssl,
                    q=q_ref[0, pl.ds(p0 + qrow, DIL_QBLK, stride=dil), :].astype(BF16),
                    kcat=kcat.astype(BF16),
                    vaug=vaug.astype(BF16),
                    band=jnp.abs((c0 + qi) - (ks + ki)) <= C_HALF,
                    acc=acc_ref[ssl, :], m=m_ref[ssl, :], l=l_ref[ssl, :]))
            for b in blk:
                b["s"] = lax.dot_general(b["q"], b["kcat"], (((1,), (1,)), ((), ())),
                                         preferred_element_type=F32)
            for b in blk:
                s0 = jnp.where(b["band"], b["s"][:, :DIL_KBLK], NEG_BIG)
                s1 = jnp.where(b["band"], b["s"][:, DIL_KBLK:], NEG_BIG)
                t0 = jnp.maximum(s0[:, :pair], s0[:, pair:])
                t1 = jnp.maximum(s1[:, :pair], s1[:, pair:])
                m0 = jnp.max(jnp.where(h0, jnp.maximum(t0, b["m"]), t0), axis=1, keepdims=True)
                m1 = jnp.max(jnp.where(h0, t1, jnp.maximum(t1, b["m"])), axis=1, keepdims=True)
                m_new = jnp.where(h0, m0, m1)
                b["p"] = jnp.concatenate([jnp.exp2(s0 - m0), jnp.exp2(s1 - m1)],
                                         axis=1).astype(BF16)
                b["alpha"] = jnp.exp2(b["m"] - m_new)
                b["m"] = m_new
            for b in blk:
                pv = jnp.dot(b["p"], b["vaug"], preferred_element_type=F32)
                b["acc"] = b["alpha"] * b["acc"] + pv[:, :pair]
                b["l"] = b["alpha"] * b["l"] + pv[:, pair:]
            for b in blk:
                acc_ref[b["ssl"], :] = b["acc"]
                m_ref[b["ssl"], :] = b["m"]
                l_ref[b["ssl"], :] = b["l"]
            return carry

        lax.fori_loop(0, dil * blocks // DIL_UNROLL, body, 0)

    o_ref[0] = acc_ref[...] / l_ref[...]


def _dilated_attention(qc, kc, vc):
    bsz, seq, width = qc.shape
    sup = min(DIL_SUPER, seq)
    pair = 2 * C_DIM
    in_spec = pl.BlockSpec((1, seq, pair), lambda bi, hp, sb: (bi, 0, hp))
    return pl.pallas_call(
        functools.partial(_dilated_kernel, seq=seq, sup=sup),
        grid=(bsz, width // pair, seq // sup),
        in_specs=[in_spec, in_spec, in_spec],
        out_specs=pl.BlockSpec((1, sup, pair), lambda bi, hp, sb: (bi, sb, hp)),
        out_shape=jax.ShapeDtypeStruct((bsz, seq, width), F32),
        scratch_shapes=[pltpu.VMEM((sup, pair), F32)] * 3,
        compiler_params=_cparams(("parallel", "parallel", "arbitrary")),
        name="dilated",
    )(qc, kc, vc)


def _outproj_kernel(x_ref, oa_ref, od_ref, oc_ref, woa_ref, wob_ref, woc_ref, lam_ref, dg_ref,
                    g_ref, b_ref, o_ref, *, lambda_init):
    ts = x_ref.shape[-1]
    lf = lam_ref[...]
    lam = (jnp.exp(jnp.sum(lf[0:1] * lf[1:2], axis=1, keepdims=True))
           - jnp.exp(jnp.sum(lf[2:3] * lf[3:4], axis=1, keepdims=True)) + lambda_init)
    def tile(ref):
        return jnp.concatenate([ref[0, :, c] for c in range(ref.shape[2])], axis=-1)

    od = tile(od_ref).reshape(B_HEADS, 2, B_V, ts)
    ob = od[:, 0] - lam * od[:, 1]
    ms = jnp.mean(ob * ob, axis=1, keepdims=True)
    ob = ob * lax.rsqrt(ms + NORM_EPS) * dg_ref[...] * (1.0 - lambda_init)
    ob = ob.reshape(B_HEADS * B_V, ts).astype(BF16)
    oa = tile(oa_ref).reshape(A_HEADS * A_V, ts).astype(BF16)
    y = jnp.dot(woa_ref[...], oa, preferred_element_type=F32)
    y += jnp.dot(wob_ref[...], ob, preferred_element_type=F32)
    y += lax.dot_general(woc_ref[...], oc_ref[0].astype(BF16), (((1,), (1,)), ((), ())),
                         preferred_element_type=F32)
    o_ref[0] = _layer_norm_rows(ALPHA * x_ref[0] + y, g_ref[...], b_ref[...])


def _outproj_layer(xT, oaT, odT, oc, woaT, wobT, wocT, lam, dg, g, b, lambda_init):
    bsz, _, seq = xT.shape
    ts = SEQ_TILE
    full = lambda a: pl.BlockSpec(a.shape, lambda bi, i: (0,) * a.ndim)
    col = lambda rows: pl.BlockSpec((1, rows, ts), lambda bi, i: (bi, 0, i))
    heads, _, dv, tc = oaT.shape[1:]
    chunks = pl.BlockSpec((1, heads, ts // tc, dv, tc), lambda bi, i: (bi, 0, i, 0, 0))
    return pl.pallas_call(
        functools.partial(_outproj_kernel, lambda_init=lambda_init),
        grid=(bsz, seq // ts),
        in_specs=[
            col(D_MODEL), chunks, chunks,
            pl.BlockSpec((1, ts, C_HEADS * C_DIM), lambda bi, i: (bi, i, 0)),
            full(woaT), full(wobT), full(wocT), full(lam), full(dg), full(g), full(b),
        ],
        out_specs=col(D_MODEL),
        out_shape=jax.ShapeDtypeStruct(xT.shape, F32),
        compiler_params=_cparams(("parallel", "parallel")),
        name="outproj",
    )(xT, oaT, odT, oc, woaT, wobT, wocT, lam, dg, g, b)


def _rope_tables(seq, dim):
    inv = 1.0 / (ROPE_THETA ** (jnp.arange(0, dim, 2, dtype=F32) / dim))
    ang = inv[:, None] * jnp.arange(seq, dtype=F32)[None, :]
    return jnp.cos(ang), jnp.sin(ang)


def _prep_layer(i, ln_g, ln_b, ffn_w_gate, ffn_w_up, ffn_w_down, w_in, mla_q_norm, mla_kv_norm,
                mla_w_uq, mla_w_ukv, diff_lambda, diff_subln, w_out):
    tf = D_FF // FF_CHUNKS
    col = lambda v: v.astype(F32)[:, None]
    ffn = []
    for s in range(2):
        wg = ffn_w_gate[i, s].T.reshape(FF_CHUNKS, tf, D_MODEL)
        wu = ffn_w_up[i, s].T.reshape(FF_CHUNKS, tf, D_MODEL)
        wgu = jnp.concatenate([wg, wu], axis=1).astype(BF16)
        wd = ffn_w_down[i, s].T.reshape(D_MODEL, FF_CHUNKS, tf).transpose(1, 0, 2).astype(BF16)
        ffn.append((wgu, wd))
    woT = w_out[i].T.astype(BF16)
    na, nb = A_HEADS * A_V, B_HEADS * B_V
    return dict(
        ffn=ffn,
        ln=[(col(ln_g[i, s]), col(ln_b[i, s])) for s in range(3)],
        winT=w_in[i].T.astype(BF16),
        gq=col(mla_q_norm[i]), gkv=col(mla_kv_norm[i]),
        wuqT=mla_w_uq[i].T.astype(BF16), wukvT=mla_w_ukv[i].T.astype(BF16),
        woaT=woT[:, :na], wobT=woT[:, na:na + nb], wocT=woT[:, na + nb:],
        lam=diff_lambda[i].astype(F32), dg=col(diff_subln[i]),
    )


def _run(x, layers):
    bsz, seq, _ = x.shape
    assert seq % DIL_SUPER == 0 and seq // C_BRANCHES[-1][1] >= DIL_KBLK, seq
    assert seq % SEQ_TILE == 0 and seq % ATTN_TQ == 0 and FF_CHUNKS == 2
    rope = _rope_tables(seq, A_ROPE) + _rope_tables(seq, C_DIM)
    xT = jnp.swapaxes(x, 1, 2)
    for i, w in enumerate(layers):
        lambda_init = 0.8 - 0.6 * math.exp(-0.3 * i)
        xT = _ffn_layer(xT, *w["ffn"][0], *w["ln"][0])
        qaT, ka, vaT, qdT, kd, vdT, qc, kc, vc = _inproj_layer(
            xT, w["winT"], w["gq"], w["gkv"], w["wuqT"], w["wukvT"], rope)
        oaT = _attention(qaT, ka, vaT, heads=A_HEADS, shared_qk=False, v_heads=A_HEADS)
        odT = _attention(qdT, kd, vdT, heads=2 * B_HEADS, shared_qk=True, v_heads=B_HEADS)
        oc = _dilated_attention(qc, kc, vc)
        xT = _outproj_layer(xT, oaT, odT, oc, w["woaT"], w["wobT"], w["wocT"], w["lam"], w["dg"],
                            *w["ln"][1], lambda_init)
        xT = _ffn_layer(xT, *w["ffn"][1], *w["ln"][2])
    return jnp.swapaxes(xT, 1, 2)


def kernel(x_prompt, x_sample, ln_g, ln_b, ffn_w_gate, ffn_w_up, ffn_w_down, w_in, mla_q_norm,
           mla_kv_norm, mla_w_uq, mla_w_ukv, diff_lambda, diff_subln, w_out):
    params = (ln_g, ln_b, ffn_w_gate, ffn_w_up, ffn_w_down, w_in, mla_q_norm, mla_kv_norm,
              mla_w_uq, mla_w_ukv, diff_lambda, diff_subln, w_out)
    layers = [_prep_layer(i, *params) for i in range(DEPTH)]
    return _run(x_prompt, layers), _run(x_sample, layers)
```

```python
import functools
import math

import jax
import jax.numpy as jnp
from jax import lax
from jax.experimental import pallas as pl
from jax.experimental.pallas import tpu as pltpu

F32 = jnp.float32
BF16 = jnp.bfloat16

D_MODEL = 1024
D_FF = 2816
DEPTH = 4
ROPE_THETA = 10000.0
NORM_EPS = 1e-5
A_HEADS, A_NOPE, A_ROPE, A_V = 8, 64, 32, 64
A_Q_LORA, A_KV_LORA = 384, 256
A_QK_PAD = 128
B_HEADS, B_D = 4, 32
B_V = 2 * B_D
C_HEADS, C_DIM = 4, 64
C_BRANCHES = ((128, 1), (512, 4), (2048, 16))
C_HALF = 64
IN_SPLITS = (A_Q_LORA, A_KV_LORA, A_ROPE, 256, 256, 256, 256, 256, 256)
IN_OFFS = tuple(int(sum(IN_SPLITS[:i])) for i in range(len(IN_SPLITS) + 1))
ALPHA = (2 * DEPTH) ** 0.25
LOG2E = math.log2(math.e)

VMEM_LIMIT_BYTES = 56 * 1024 * 1024
SEQ_TILE = 512
FF_CHUNKS = 2
ATTN_TQ = 2048
ATTN_TK = 512
ATTN_CHAINS = 8
ATTN_UNROLL = 4
PIPE_SLOTS = 2
SUM_ROWS = 16
DIL_QBLK = 128
DIL_KBLK = 256
DIL_SUPER = 2048
DIL_UNROLL = 4
NEG_BIG = -1e30
MAX_SHIFT_LAG = 64.0


def _cparams(sem):
    return pltpu.CompilerParams(dimension_semantics=sem, vmem_limit_bytes=VMEM_LIMIT_BYTES)


def _layer_norm_rows(y, g, b):
    mu = jnp.mean(y, axis=0, keepdims=True)
    d = y - mu
    var = jnp.mean(d * d, axis=0, keepdims=True)
    return d * lax.rsqrt(var + NORM_EPS) * g + b


def _rms_norm_rows(y, g):
    ms = jnp.mean(y * y, axis=0, keepdims=True)
    return y * lax.rsqrt(ms + NORM_EPS) * g


def _rope_rows(x, c, s, groups, dim):
    half = dim // 2
    t = x.shape[-1]
    xg = x.reshape(groups, 2, half, t)
    x1, x2 = xg[:, 0], xg[:, 1]
    o1 = x1 * c - x2 * s
    o2 = x2 * c + x1 * s
    return jnp.stack([o1, o2], axis=1).reshape(groups * dim, t)


def _ffn_kernel(x_ref, wgu_ref, wd_ref, g_ref, b_ref, o_ref, y_ref, *, tf):
    @pl.when(pl.program_id(0) == 0)
    def _():
        y_ref[...] = jnp.zeros(y_ref.shape, F32)

    xb = x_ref[0].astype(BF16)
    gus = [jnp.dot(wgu_ref[j], xb, preferred_element_type=F32) for j in range(FF_CHUNKS)]
    h = jnp.concatenate([(gu[:tf] * jax.nn.sigmoid(gu[:tf]) * gu[tf:]).astype(BF16)
                         for gu in gus], axis=0)
    o_ref[0] = _layer_norm_rows(y_ref[...], g_ref[...], b_ref[...])
    half = D_MODEL // 2
    for r in range(0, D_MODEL, half):
        ff = jnp.dot(wd_ref[r:r + half, :], h, preferred_element_type=F32)
        y_ref[r:r + half, :] = ALPHA * x_ref[0, r:r + half, :] + 0.5 * ff


def _ffn_layer(xT, wguT, wdT, g, b):
    bsz, _, seq = xT.shape
    ts = SEQ_TILE
    tf = D_FF // FF_CHUNKS
    nt = seq // ts
    tiles = bsz * nt

    def x_map(t):
        tt = jnp.minimum(t, tiles - 1)
        return (tt // nt, 0, tt % nt)

    def o_map(t):
        tt = jnp.maximum(t - 1, 0)
        return (tt // nt, 0, tt % nt)

    resident = lambda a: pl.BlockSpec(a.shape, lambda t: (0,) * a.ndim,
                                      pipeline_mode=pl.Buffered(1))
    return pl.pallas_call(
        functools.partial(_ffn_kernel, tf=tf),
        grid=(tiles + 1,),
        in_specs=[pl.BlockSpec((1, D_MODEL, ts), x_map),
                  resident(wguT), resident(wdT), resident(g), resident(b)],
        out_specs=pl.BlockSpec((1, D_MODEL, ts), o_map),
        out_shape=jax.ShapeDtypeStruct(xT.shape, F32),
        scratch_shapes=[pltpu.VMEM((D_MODEL, ts), F32)],
        compiler_params=_cparams(("arbitrary",)),
        name="ffn",
    )(xT, wguT, wdT, g, b)


def _inproj_kernel(x_ref, win_ref, gq_ref, gkv_ref, wuq_ref, wukv_ref,
                   c32_ref, s32_ref, c64_ref, s64_ref,
                   qa_ref, ka_ref, va_ref, qd_ref, kd_ref, vd_ref, qc_ref, kc_ref, vc_ref):
    xb = x_ref[0].astype(BF16)
    ts = xb.shape[-1]
    c32, s32 = c32_ref[...], s32_ref[...]
    c64, s64 = c64_ref[...], s64_ref[...]

    def proj(lo, hi):
        return jnp.dot(win_ref[lo:hi, :], xb, preferred_element_type=F32)

    nq = _rms_norm_rows(proj(IN_OFFS[0], IN_OFFS[1]), gq_ref[...]).astype(BF16)
    qa = jnp.dot(wuq_ref[...], nq, preferred_element_type=F32)
    nkv = _rms_norm_rows(proj(IN_OFFS[1], IN_OFFS[2]), gkv_ref[...]).astype(BF16)
    kva = jnp.dot(wukv_ref[...], nkv, preferred_element_type=F32)
    kpe = _rope_rows(proj(IN_OFFS[2], IN_OFFS[3]), c32, s32, 1, A_ROPE)
    qscale = (A_NOPE + A_ROPE) ** -0.5 * LOG2E
    zpad = jnp.zeros((A_QK_PAD - A_NOPE - A_ROPE, ts), F32)
    dqk = A_NOPE + A_ROPE
    for h in range(A_HEADS):
        qh = qa[h * dqk:(h + 1) * dqk]
        q_pe = _rope_rows(qh[A_NOPE:], c32, s32, 1, A_ROPE)
        qa_ref[0, h, 0] = (jnp.concatenate([qh[:A_NOPE], q_pe, zpad], axis=0) * qscale).astype(BF16)
        kvh = kva[h * (A_NOPE + A_V):(h + 1) * (A_NOPE + A_V)]
        kT = jnp.concatenate([kvh[:A_NOPE], kpe, zpad], axis=0)
        ka_ref[0, h] = kT.T.astype(BF16)
        va_ref[0, h, 0] = kvh[A_NOPE:].astype(BF16)

    pb = proj(IN_OFFS[3], IN_OFFS[6])
    qd = _rope_rows(pb[0:256], c32, s32, 2 * B_HEADS, B_D) * (B_D ** -0.5 * LOG2E)
    qd_ref[0, 0] = qd.astype(BF16)
    kd = _rope_rows(pb[256:512], c32, s32, 2 * B_HEADS, B_D)
    kd_ref[0] = kd.T.astype(BF16)
    for h in range(B_HEADS):
        vd_ref[0, h, 0] = pb[512 + h * B_V:512 + (h + 1) * B_V].astype(BF16)

    pc = proj(IN_OFFS[6], IN_OFFS[9])
    qc = _rope_rows(pc[0:256], c64, s64, C_HEADS, C_DIM) * (C_DIM ** -0.5 * LOG2E)
    kc = _rope_rows(pc[256:512], c64, s64, C_HEADS, C_DIM)
    qc_ref[0] = qc.T
    kc_ref[0] = kc.T
    vc_ref[0] = pc[512:768].T


def _inproj_layer(xT, winT, gq, gkv, wuqT, wukvT, rope):
    bsz, _, seq = xT.shape
    ts = SEQ_TILE
    nt = seq // ts
    c32, s32, c64, s64 = rope
    full = lambda shape: pl.BlockSpec(shape, lambda bi, i: (0,) * len(shape))
    tab = lambda rows: pl.BlockSpec((rows, ts), lambda bi, i: (0, i))
    out_shape = (
        jax.ShapeDtypeStruct((bsz, A_HEADS, nt, A_QK_PAD, ts), BF16),
        jax.ShapeDtypeStruct((bsz, A_HEADS, seq, A_QK_PAD), BF16),
        jax.ShapeDtypeStruct((bsz, A_HEADS, nt, A_V, ts), BF16),
        jax.ShapeDtypeStruct((bsz, nt, 2 * B_HEADS * B_D, ts), BF16),
        jax.ShapeDtypeStruct((bsz, seq, 2 * B_HEADS * B_D), BF16),
        jax.ShapeDtypeStruct((bsz, B_HEADS, nt, B_V, ts), BF16),
        jax.ShapeDtypeStruct((bsz, seq, C_HEADS * C_DIM), F32),
        jax.ShapeDtypeStruct((bsz, seq, C_HEADS * C_DIM), F32),
        jax.ShapeDtypeStruct((bsz, seq, C_HEADS * C_DIM), F32),
    )
    out_specs = (
        pl.BlockSpec((1, A_HEADS, 1, A_QK_PAD, ts), lambda bi, i: (bi, 0, i, 0, 0)),
        pl.BlockSpec((1, A_HEADS, ts, A_QK_PAD), lambda bi, i: (bi, 0, i, 0)),
        pl.BlockSpec((1, A_HEADS, 1, A_V, ts), lambda bi, i: (bi, 0, i, 0, 0)),
        pl.BlockSpec((1, 1, 256, ts), lambda bi, i: (bi, i, 0, 0)),
        pl.BlockSpec((1, ts, 256), lambda bi, i: (bi, i, 0)),
        pl.BlockSpec((1, B_HEADS, 1, B_V, ts), lambda bi, i: (bi, 0, i, 0, 0)),
        pl.BlockSpec((1, ts, 256), lambda bi, i: (bi, i, 0)),
        pl.BlockSpec((1, ts, 256), lambda bi, i: (bi, i, 0)),
        pl.BlockSpec((1, ts, 256), lambda bi, i: (bi, i, 0)),
    )
    return pl.pallas_call(
        _inproj_kernel,
        grid=(bsz, nt),
        in_specs=[
            pl.BlockSpec((1, D_MODEL, ts), lambda bi, i: (bi, 0, i)),
            full(winT.shape), full(gq.shape), full(gkv.shape), full(wuqT.shape), full(wukvT.shape),
            tab(A_ROPE // 2), tab(A_ROPE // 2), tab(C_DIM // 2), tab(C_DIM // 2),
        ],
        out_specs=out_specs,
        out_shape=out_shape,
        compiler_params=_cparams(("parallel", "parallel")),
        name="inproj",
    )(xT, winT, gq, gkv, wuqT, wukvT, c32, s32, c64, s64)


def _attn_kernel(q_ref, k_ref, v_ref, o_ref, s_ref, p_ref, *, tk, nk, group_rows):
    chains, tc = s_ref.shape[1], s_ref.shape[3]
    qv = q_ref.at[0] if len(q_ref.shape) == 4 else q_ref.at[0, 0]
    kv = k_ref.at[0] if len(k_ref.shape) == 3 else k_ref.at[0, 0]
    ts = qv.shape[-1]
    assert ts % tc == 0 and (chains * tc) % ts == 0

    def query_tile(qi, _):
        args = (qi, qv, kv, v_ref, o_ref, s_ref, p_ref)
        jump = _attn_query_tile_fast(*args, tk=tk, nk=nk, group_rows=group_rows)

        @pl.when(jump > MAX_SHIFT_LAG)
        def _():
            _attn_query_tile(*args, tk=tk, nk=nk, group_rows=group_rows)
        return 0

    lax.fori_loop(0, (qv.shape[0] * ts) // (chains * tc), query_tile, 0)


def _load_queries(qi, qv, chains, tc, group_rows):
    ts = qv.shape[-1]
    qs = []
    for c in range(chains):
        q = qv[qi * (chains * tc // ts) + (c * tc) // ts, :, (c * tc) % ts:(c * tc) % ts + tc]
        if group_rows:
            rows = lax.broadcasted_iota(jnp.int32, q.shape, 0)
            lo = pl.program_id(1) * group_rows
            q = jnp.where((rows >= lo) & (rows < lo + group_rows), q, jnp.zeros_like(q))
        qs.append(q)
    return qs


def _attn_query_tile_fast(qi, qv, kv, v_ref, o_ref, s_ref, p_ref, *, tk, nk, group_rows):
    chains, tc = s_ref.shape[1], s_ref.shape[3]
    qs = _load_queries(qi, qv, chains, tc, group_rows)
    dv = v_ref.shape[-2]
    ones_rows = (lax.broadcasted_iota(jnp.int32, (SUM_ROWS, tk), 0) == 0).astype(BF16)
    vsub = v_ref.shape[-1] // tk

    def scores(i):
        kc = kv[pl.ds(pl.multiple_of(i * tk, tk), tk), :]
        return [jnp.dot(kc, qs[c], preferred_element_type=F32) for c in range(chains)]

    def value_matmul(i, slot):
        blk = i // vsub if isinstance(i, int) else lax.div(i, vsub)
        part = slot % vsub
        vc = jnp.concatenate([v_ref[0, 0, blk, :, part * tk:(part + 1) * tk], ones_rows], axis=0)
        return [jnp.dot(vc, p_ref[slot, c], preferred_element_type=F32) for c in range(chains)]

    def step(i, carry, slot):
        ss = scores(i)
        pvs = value_matmul(i - 1, 1 - slot)
        out = []
        for c, (shift, acc, a1, a2, lag) in enumerate(carry):
            cmax = jnp.max(ss[c], axis=0, keepdims=True)
            p_ref[slot, c] = jnp.exp2(ss[c] - shift).astype(BF16)
            new_shift = jnp.maximum(shift, cmax)
            out.append((new_shift, a1 * acc + pvs[c], a2, jnp.exp2(shift - new_shift),
                        jnp.maximum(lag, cmax - shift)))
        return tuple(out)

    carry = []
    for c, s in enumerate(scores(0)):
        shift = jnp.max(s, axis=0, keepdims=True)
        p_ref[0, c] = jnp.exp2(s - shift).astype(BF16)
        one = jnp.ones((1, tc), F32)
        carry.append((shift, jnp.zeros((dv + SUM_ROWS, tc), F32), one, one, jnp.zeros((1, tc), F32)))
    assert ATTN_UNROLL % PIPE_SLOTS == 0 and PIPE_SLOTS == 2 and nk % 2 == 0
    lead = 2 + (nk - 2) % ATTN_UNROLL
    carry = tuple(carry)
    for i in range(1, lead):
        carry = step(i, carry, i % PIPE_SLOTS)

    def steps(j, carry):
        for u in range(ATTN_UNROLL):
            carry = step(ATTN_UNROLL * j + lead + u, carry, u % PIPE_SLOTS)
        return carry

    carry = lax.fori_loop(0, (nk - lead) // ATTN_UNROLL, steps, carry)
    pvs = value_matmul(nk - 1, (nk - 1) % PIPE_SLOTS)
    jump = jnp.zeros((1, tc), F32)
    for c, (_, acc, a1, _, lag) in enumerate(carry):
        acc = a1 * acc + pvs[c]
        o_ref[0, 0, qi * chains + c] = acc[:dv] / acc[dv:dv + 1]
        jump = jnp.maximum(jump, lag)
    return jnp.max(jump)


def _attn_query_tile(qi, qv, kv, v_ref, o_ref, s_ref, p_ref, *, tk, nk, group_rows):
    chains, tc = s_ref.shape[1], s_ref.shape[3]
    qs = _load_queries(qi, qv, chains, tc, group_rows)

    dv = v_ref.shape[-2]
    ones_rows = (lax.broadcasted_iota(jnp.int32, (SUM_ROWS, tk), 0) == 0).astype(BF16)

    def score_matmul(i, slot):
        start = pl.multiple_of(i * tk, tk)
        kc = kv[pl.ds(start, tk), :]
        cmax = []
        for c in range(chains):
            s = jnp.dot(kc, qs[c], preferred_element_type=F32)
            s_ref[slot, c] = s
            cmax.append(jnp.max(s, axis=0, keepdims=True))
        return tuple(cmax)

    vsub = v_ref.shape[-1] // tk
    assert PIPE_SLOTS % vsub == 0

    def value_matmul(i, slot, carry):
        part = slot % vsub
        blk = i // vsub if isinstance(i, int) else lax.div(i, vsub)
        vc = jnp.concatenate([v_ref[0, 0, blk, :, part * tk:(part + 1) * tk], ones_rows],
                             axis=0)
        pvs = [jnp.dot(vc, p_ref[slot, c], preferred_element_type=F32) for c in range(chains)]
        return tuple((m, alpha * acc + pvs[c], alpha) for c, (m, acc, alpha) in enumerate(carry))

    def step(i, carry, cmax, slot, first=False, last=False):
        nxt, prv = (slot + 1) % PIPE_SLOTS, (slot - 1) % PIPE_SLOTS
        cmax_next = None if last else score_matmul(i + 1, nxt)
        if not first:
            carry = value_matmul(i - 1, prv, carry)
        out = []
        for c, (m, acc, _) in enumerate(carry):
            m_new = jnp.maximum(m, cmax[c])
            p_ref[slot, c] = jnp.exp2(s_ref[slot, c] - m_new).astype(BF16)
            out.append((m_new, acc, jnp.exp2(m - m_new)))
        return tuple(out), cmax_next

    assert ATTN_UNROLL % PIPE_SLOTS == 0
    lead = 2 + (nk - 4) % ATTN_UNROLL

    def steps(j, state):
        carry, cmax = state
        for u in range(ATTN_UNROLL):
            carry, cmax = step(ATTN_UNROLL * j + lead + u, carry, cmax, (lead + u) % PIPE_SLOTS)
        return carry, cmax

    carry = tuple((jnp.full((1, tc), NEG_BIG, F32), jnp.zeros((dv + SUM_ROWS, tc), F32),
                   jnp.ones((1, tc), F32)) for _ in range(chains))
    cmax = score_matmul(0, 0)
    for i in range(lead):
        carry, cmax = step(i, carry, cmax, i % PIPE_SLOTS, first=(i == 0))
    carry, cmax = lax.fori_loop(0, (nk - 2 - lead) // ATTN_UNROLL, steps, (carry, cmax))
    carry, cmax = step(nk - 2, carry, cmax, (nk - 2) % PIPE_SLOTS)
    carry, _ = step(nk - 1, carry, cmax, (nk - 1) % PIPE_SLOTS, last=True)
    carry = value_matmul(nk - 1, (nk - 1) % PIPE_SLOTS, carry)
    for c, (_, acc, _) in enumerate(carry):
        o_ref[0, 0, qi * chains + c] = acc[:dv] / acc[dv:dv + 1]


def _attention(qT, k, vT, *, heads, shared_qk, v_heads):
    bsz, nt, ts = qT.shape[0], qT.shape[-3], qT.shape[-1]
    seq = nt * ts
    nv, dv, tv = vT.shape[2], vT.shape[3], vT.shape[4]
    tk, tc = ATTN_TK, ATTN_TQ // ATTN_CHAINS
    nk = seq // tk
    vrep = heads // v_heads
    if shared_qk:
        dq_all = qT.shape[2]
        q_spec = pl.BlockSpec((1, nt, dq_all, ts), lambda bi, h: (bi, 0, 0, 0))
        k_spec = pl.BlockSpec((1, seq, dq_all), lambda bi, h: (bi, 0, 0))
        group_rows = dq_all // heads
    else:
        dq = qT.shape[3]
        q_spec = pl.BlockSpec((1, 1, nt, dq, ts), lambda bi, h: (bi, h, 0, 0, 0))
        k_spec = pl.BlockSpec((1, 1, seq, dq), lambda bi, h: (bi, h, 0, 0))
        group_rows = 0
    return pl.pallas_call(
        functools.partial(_attn_kernel, tk=tk, nk=nk, group_rows=group_rows),
        scratch_shapes=[pltpu.VMEM((PIPE_SLOTS, ATTN_CHAINS, tk, tc), F32),
                        pltpu.VMEM((PIPE_SLOTS, ATTN_CHAINS, tk, tc), BF16)],
        grid=(bsz, heads),
        in_specs=[
            q_spec, k_spec,
            pl.BlockSpec((1, 1, nv, dv, tv), lambda bi, h: (bi, h // vrep, 0, 0, 0)),
        ],
        out_specs=pl.BlockSpec((1, 1, seq // tc, dv, tc), lambda bi, h: (bi, h, 0, 0, 0)),
        out_shape=jax.ShapeDtypeStruct((bsz, heads, seq // tc, dv, tc), F32),
        compiler_params=_cparams(("parallel", "parallel")),
        name="attn_shared" if shared_qk else "attn",
    )(qT, k, vT)


def _dilated_kernel(q_ref, k_ref, v_ref, o_ref, m_ref, l_ref, acc_ref, *, seq, sup):
    sb = pl.program_id(2)
    p0 = sb * sup
    m_ref[...] = jnp.full(m_ref.shape, NEG_BIG, F32)
    l_ref[...] = jnp.zeros(l_ref.shape, F32)
    acc_ref[...] = jnp.zeros(acc_ref.shape, F32)
    pair = 2 * C_DIM
    lane = lax.broadcasted_iota(jnp.int32, (1, pair), 1)
    h0 = lane < C_DIM
    qi = lax.broadcasted_iota(jnp.int32, (DIL_QBLK, 1), 0)
    ki = lax.broadcasted_iota(jnp.int32, (1, DIL_KBLK), 1)
    ones0 = jnp.broadcast_to(h0.astype(F32), (DIL_KBLK, pair))
    ones1 = 1.0 - ones0

    for _, dil in C_BRANCHES:
        cls_len = seq // dil
        blocks = sup // (DIL_QBLK * dil)

        def body(tt, carry, dil=dil, cls_len=cls_len, blocks=blocks):
            blk = []
            for u in range(DIL_UNROLL):
                t = tt * DIL_UNROLL + u
                r = lax.div(t, blocks)
                jb = lax.rem(t, blocks)
                c0 = sb * (sup // dil) + DIL_QBLK * jb
                ks = jnp.clip(c0 - C_HALF, 0, cls_len - DIL_KBLK)
                qrow = r + dil * DIL_QBLK * jb
                ssl = pl.ds(qrow, DIL_QBLK, stride=dil)
                ksl = pl.ds(r + dil * ks, DIL_KBLK, stride=dil)
                kf = k_ref[0, ksl, :]
                vf = v_ref[0, ksl, :]
                kcat = jnp.concatenate([jnp.where(h0, kf, 0.0), jnp.where(h0, 0.0, kf)], axis=0)
                vaug = jnp.concatenate(
                    [jnp.concatenate([jnp.where(h0, vf, 0.0), ones0], axis=1),
                     jnp.concatenate([jnp.where(h0, 0.0, vf), ones1], axis=1)], axis=0)
                blk.append(dict(
                    ssl=ssl,
                    q=q_ref[0, pl.ds(p0 + qrow, DIL_QBLK, stride=dil), :].astype(BF16),
                    kcat=kcat.astype(BF16),
                    vaug=vaug.astype(BF16),
                    band=jnp.abs((c0 + qi) - (ks + ki)) <= C_HALF,
                    acc=acc_ref[ssl, :], m=m_ref[ssl, :], l=l_ref[ssl, :]))
            for b in blk:
                b["s"] = lax.dot_general(b["q"], b["kcat"], (((1,), (1,)), ((), ())),
                                         preferred_element_type=F32)
            for b in blk:
                s0 = jnp.where(b["band"], b["s"][:, :DIL_KBLK], NEG_BIG)
                s1 = jnp.where(b["band"], b["s"][:, DIL_KBLK:], NEG_BIG)
                t0 = jnp.maximum(s0[:, :pair], s0[:, pair:])
                t1 = jnp.maximum(s1[:, :pair], s1[:, pair:])
                m0 = jnp.max(jnp.where(h0, jnp.maximum(t0, b["m"]), t0), axis=1, keepdims=True)
                m1 = jnp.max(jnp.where(h0, t1, jnp.maximum(t1, b["m"])), axis=1, keepdims=True)
                m_new = jnp.where(h0, m0, m1)
                b["p"] = jnp.concatenate([jnp.exp2(s0 - m0), jnp.exp2(s1 - m1)],
                                         axis=1).astype(BF16)
                b["alpha"] = jnp.exp2(b["m"] - m_new)
                b["m"] = m_new
            for b in blk:
                pv = jnp.dot(b["p"], b["vaug"], preferred_element_type=F32)
                b["acc"] = b["alpha"] * b["acc"] + pv[:, :pair]
                b["l"] = b["alpha"] * b["l"] + pv[:, pair:]
            for b in blk:
                acc_ref[b["ssl"], :] = b["acc"]
                m_ref[b["ssl"], :] = b["m"]
                l_ref[b["ssl"], :] = b["l"]
            return carry

        lax.fori_loop(0, dil * blocks // DIL_UNROLL, body, 0)

    o_ref[0] = acc_ref[...] / l_ref[...]


def _dilated_attention(qc, kc, vc):
    bsz, seq, width = qc.shape
    sup = min(DIL_SUPER, seq)
    pair = 2 * C_DIM
    in_spec = pl.BlockSpec((1, seq, pair), lambda bi, hp, sb: (bi, 0, hp))
    return pl.pallas_call(
        functools.partial(_dilated_kernel, seq=seq, sup=sup),
        grid=(bsz, width // pair, seq // sup),
        in_specs=[in_spec, in_spec, in_spec],
        out_specs=pl.BlockSpec((1, sup, pair), lambda bi, hp, sb: (bi, sb, hp)),
        out_shape=jax.ShapeDtypeStruct((bsz, seq, width), F32),
        scratch_shapes=[pltpu.VMEM((sup, pair), F32)] * 3,
        compiler_params=_cparams(("parallel", "parallel", "arbitrary")),
        name="dilated",
    )(qc, kc, vc)


def _outproj_kernel(x_ref, oa_ref, od_ref, oc_ref, woa_ref, wob_ref, woc_ref, lam_ref, dg_ref,
                    g_ref, b_ref, o_ref, *, lambda_init):
    ts = x_ref.shape[-1]
    lf = lam_ref[...]
    lam = (jnp.exp(jnp.sum(lf[0:1] * lf[1:2], axis=1, keepdims=True))
           - jnp.exp(jnp.sum(lf[2:3] * lf[3:4], axis=1, keepdims=True)) + lambda_init)
    def tile(ref):
        return jnp.concatenate([ref[0, :, c] for c in range(ref.shape[2])], axis=-1)

    od = tile(od_ref).reshape(B_HEADS, 2, B_V, ts)
    ob = od[:, 0] - lam * od[:, 1]
    ms = jnp.mean(ob * ob, axis=1, keepdims=True)
    ob = ob * lax.rsqrt(ms + NORM_EPS) * dg_ref[...] * (1.0 - lambda_init)
    ob = ob.reshape(B_HEADS * B_V, ts).astype(BF16)
    oa = tile(oa_ref).reshape(A_HEADS * A_V, ts).astype(BF16)
    y = jnp.dot(woa_ref[...], oa, preferred_element_type=F32)
    y += jnp.dot(wob_ref[...], ob, preferred_element_type=F32)
    y += lax.dot_general(woc_ref[...], oc_ref[0].astype(BF16), (((1,), (1,)), ((), ())),
                         preferred_element_type=F32)
    o_ref[0] = _layer_norm_rows(ALPHA * x_ref[0] + y, g_ref[...], b_ref[...])


def _outproj_layer(xT, oaT, odT, oc, woaT, wobT, wocT, lam, dg, g, b, lambda_init):
    bsz, _, seq = xT.shape
    ts = SEQ_TILE
    full = lambda a: pl.BlockSpec(a.shape, lambda bi, i: (0,) * a.ndim)
    col = lambda rows: pl.BlockSpec((1, rows, ts), lambda bi, i: (bi, 0, i))
    heads, _, dv, tc = oaT.shape[1:]
    chunks = pl.BlockSpec((1, heads, ts // tc, dv, tc), lambda bi, i: (bi, 0, i, 0, 0))
    return pl.pallas_call(
        functools.partial(_outproj_kernel, lambda_init=lambda_init),
        grid=(bsz, seq // ts),
        in_specs=[
            col(D_MODEL), chunks, chunks,
            pl.BlockSpec((1, ts, C_HEADS * C_DIM), lambda bi, i: (bi, i, 0)),
            full(woaT), full(wobT), full(wocT), full(lam), full(dg), full(g), full(b),
        ],
        out_specs=col(D_MODEL),
        out_shape=jax.ShapeDtypeStruct(xT.shape, F32),
        compiler_params=_cparams(("parallel", "parallel")),
        name="outproj",
    )(xT, oaT, odT, oc, woaT, wobT, wocT, lam, dg, g, b)


def _rope_tables(seq, dim):
    inv = 1.0 / (ROPE_THETA ** (jnp.arange(0, dim, 2, dtype=F32) / dim))
    ang = inv[:, None] * jnp.arange(seq, dtype=F32)[None, :]
    return jnp.cos(ang), jnp.sin(ang)


def _prep_layer(i, ln_g, ln_b, ffn_w_gate, ffn_w_up, ffn_w_down, w_in, mla_q_norm, mla_kv_norm,
                mla_w_uq, mla_w_ukv, diff_lambda, diff_subln, w_out):
    tf = D_FF // FF_CHUNKS
    col = lambda v: v.astype(F32)[:, None]
    ffn = []
    for s in range(2):
        wg = ffn_w_gate[i, s].T.reshape(FF_CHUNKS, tf, D_MODEL)
        wu = ffn_w_up[i, s].T.reshape(FF_CHUNKS, tf, D_MODEL)
        wgu = jnp.concatenate([wg, wu], axis=1).astype(BF16)
        wd = ffn_w_down[i, s].T.astype(BF16)
        ffn.append((wgu, wd))
    woT = w_out[i].T.astype(BF16)
    na, nb = A_HEADS * A_V, B_HEADS * B_V
    return dict(
        ffn=ffn,
        ln=[(col(ln_g[i, s]), col(ln_b[i, s])) for s in range(3)],
        winT=w_in[i].T.astype(BF16),
        gq=col(mla_q_norm[i]), gkv=col(mla_kv_norm[i]),
        wuqT=mla_w_uq[i].T.astype(BF16), wukvT=mla_w_ukv[i].T.astype(BF16),
        woaT=woT[:, :na], wobT=woT[:, na:na + nb], wocT=woT[:, na + nb:],
        lam=diff_lambda[i].astype(F32), dg=col(diff_subln[i]),
    )


def _run(x, layers):
    bsz, seq, _ = x.shape
    assert seq % DIL_SUPER == 0 and seq // C_BRANCHES[-1][1] >= DIL_KBLK, seq
    assert seq % SEQ_TILE == 0 and seq % ATTN_TQ == 0 and FF_CHUNKS == 2
    rope = _rope_tables(seq, A_ROPE) + _rope_tables(seq, C_DIM)
    xT = jnp.swapaxes(x, 1, 2)
    for i, w in enumerate(layers):
        lambda_init = 0.8 - 0.6 * math.exp(-0.3 * i)
        xT = _ffn_layer(xT, *w["ffn"][0], *w["ln"][0])
        qaT, ka, vaT, qdT, kd, vdT, qc, kc, vc = _inproj_layer(
            xT, w["winT"], w["gq"], w["gkv"], w["wuqT"], w["wukvT"], rope)
        oaT = _attention(qaT, ka, vaT, heads=A_HEADS, shared_qk=False, v_heads=A_HEADS)
        odT = _attention(qdT, kd, vdT, heads=2 * B_HEADS, shared_qk=True, v_heads=B_HEADS)
        oc = _dilated_attention(qc, kc, vc)
        xT = _outproj_layer(xT, oaT, odT, oc, w["woaT"], w["wobT"], w["wocT"], w["lam"], w["dg"],
                            *w["ln"][1], lambda_init)
        xT = _ffn_layer(xT, *w["ffn"][1], *w["ln"][2])
    return jnp.swapaxes(xT, 1, 2)


def kernel(x_prompt, x_sample, ln_g, ln_b, ffn_w_gate, ffn_w_up, ffn_w_down, w_in, mla_q_norm,
           mla_kv_norm, mla_w_uq, mla_w_ukv, diff_lambda, diff_subln, w_out):
    params = (ln_g, ln_b, ffn_w_gate, ffn_w_up, ffn_w_down, w_in, mla_q_norm, mla_kv_norm,
              mla_w_uq, mla_w_ukv, diff_lambda, diff_subln, w_out)
    layers = [_prep_layer(i, *params) for i in range(DEPTH)]
    return _run(x_prompt, layers), _run(x_sample, layers)
```

```python
import functools
import math

import jax
import jax.numpy as jnp
from jax import lax
from jax.experimental import pallas as pl
from jax.experimental.pallas import tpu as pltpu

F32 = jnp.float32
BF16 = jnp.bfloat16

D_MODEL = 1024
D_FF = 2816
DEPTH = 4
ROPE_THETA = 10000.0
NORM_EPS = 1e-5
A_HEADS, A_NOPE, A_ROPE, A_V = 8, 64, 32, 64
A_Q_LORA, A_KV_LORA = 384, 256
A_QK_PAD = 128
B_HEADS, B_D = 4, 32
B_V = 2 * B_D
C_HEADS, C_DIM = 4, 64
C_BRANCHES = ((128, 1), (512, 4), (2048, 16))
C_HALF = 64
IN_SPLITS = (A_Q_LORA, A_KV_LORA, A_ROPE, 256, 256, 256, 256, 256, 256)
IN_OFFS = tuple(int(sum(IN_SPLITS[:i])) for i in range(len(IN_SPLITS) + 1))
ALPHA = (2 * DEPTH) ** 0.25
LOG2E = math.log2(math.e)

VMEM_LIMIT_BYTES = 56 * 1024 * 1024
SEQ_TILE = 512
FF_CHUNKS = 2
ATTN_TQ = 2048
ATTN_TK = 512
ATTN_CHAINS = 8
ATTN_UNROLL = 6
PIPE_SLOTS = 2
SUM_ROWS = 16
DIL_QBLK = 128
DIL_KBLK = 256
DIL_SUPER = 2048
DIL_UNROLL = 4
NEG_BIG = -1e30
MAX_SHIFT_LAG = 64.0


def _cparams(sem):
    return pltpu.CompilerParams(dimension_semantics=sem, vmem_limit_bytes=VMEM_LIMIT_BYTES)


def _layer_norm_rows(y, g, b):
    mu = jnp.mean(y, axis=0, keepdims=True)
    d = y - mu
    var = jnp.mean(d * d, axis=0, keepdims=True)
    return d * lax.rsqrt(var + NORM_EPS) * g + b


def _rms_norm_rows(y, g):
    ms = jnp.mean(y * y, axis=0, keepdims=True)
    return y * lax.rsqrt(ms + NORM_EPS) * g


def _rope_rows(x, c, s, groups, dim):
    half = dim // 2
    t = x.shape[-1]
    xg = x.reshape(groups, 2, half, t)
    x1, x2 = xg[:, 0], xg[:, 1]
    o1 = x1 * c - x2 * s
    o2 = x2 * c + x1 * s
    return jnp.stack([o1, o2], axis=1).reshape(groups * dim, t)


def _ffn_kernel(x_ref, wgu_ref, wd_ref, g_ref, b_ref, o_ref, y_ref, *xt_ref, tf, o_tokens):
    @pl.when(pl.program_id(0) == 0)
    def _():
        y_ref[...] = jnp.zeros(y_ref.shape, F32)

    if xt_ref:
        xt_ref[0][...] = x_ref[0].T
        x_rows = xt_ref[0]
    else:
        x_rows = x_ref.at[0]
    xb = x_rows[...].astype(BF16)
    gus = [jnp.dot(wgu_ref[j], xb, preferred_element_type=F32) for j in range(FF_CHUNKS)]
    h = jnp.concatenate([(gu[:tf] * jax.nn.sigmoid(gu[:tf]) * gu[tf:]).astype(BF16)
                         for gu in gus], axis=0)
    normed = _layer_norm_rows(y_ref[...], g_ref[...], b_ref[...])
    o_ref[0] = normed.T if o_tokens else normed
    half = D_MODEL // 2
    for r in range(0, D_MODEL, half):
        ff = jnp.dot(wd_ref[r:r + half, :], h, preferred_element_type=F32)
        y_ref[r:r + half, :] = ALPHA * x_rows[r:r + half, :] + 0.5 * ff


def _ffn_layer(x, wguT, wdT, g, b, *, x_tokens=False, o_tokens=False):
    bsz = x.shape[0]
    seq = x.shape[1] if x_tokens else x.shape[2]
    ts = SEQ_TILE
    tf = D_FF // FF_CHUNKS
    nt = seq // ts
    tiles = bsz * nt

    def tile_spec(tokens, clamp):
        def index(t):
            tt = clamp(t)
            return (tt // nt, tt % nt, 0) if tokens else (tt // nt, 0, tt % nt)
        return pl.BlockSpec((1, ts, D_MODEL) if tokens else (1, D_MODEL, ts), index)

    resident = lambda a: pl.BlockSpec(a.shape, lambda t: (0,) * a.ndim,
                                      pipeline_mode=pl.Buffered(1))
    tile = pltpu.VMEM((D_MODEL, ts), F32)
    return pl.pallas_call(
        functools.partial(_ffn_kernel, tf=tf, o_tokens=o_tokens),
        grid=(tiles + 1,),
        in_specs=[tile_spec(x_tokens, lambda t: jnp.minimum(t, tiles - 1)),
                  resident(wguT), resident(wdT), resident(g), resident(b)],
        out_specs=tile_spec(o_tokens, lambda t: jnp.maximum(t - 1, 0)),
        out_shape=jax.ShapeDtypeStruct(
            (bsz, seq, D_MODEL) if o_tokens else (bsz, D_MODEL, seq), F32),
        scratch_shapes=[tile, tile] if x_tokens else [tile],
        compiler_params=_cparams(("arbitrary",)),
        name="ffn",
    )(x, wguT, wdT, g, b)


def _inproj_kernel(x_ref, win_ref, gq_ref, gkv_ref, wuq_ref, wukv_ref,
                   c32_ref, s32_ref, c64_ref, s64_ref,
                   qa_ref, ka_ref, va_ref, qd_ref, kd_ref, vd_ref, qc_ref, kc_ref, vc_ref):
    xb = x_ref[0].astype(BF16)
    ts = xb.shape[-1]
    c32, s32 = c32_ref[...], s32_ref[...]
    c64, s64 = c64_ref[...], s64_ref[...]

    def proj(lo, hi):
        return jnp.dot(win_ref[lo:hi, :], xb, preferred_element_type=F32)

    nq = _rms_norm_rows(proj(IN_OFFS[0], IN_OFFS[1]), gq_ref[...]).astype(BF16)
    qa = jnp.dot(wuq_ref[...], nq, preferred_element_type=F32)
    nkv = _rms_norm_rows(proj(IN_OFFS[1], IN_OFFS[2]), gkv_ref[...]).astype(BF16)
    kva = jnp.dot(wukv_ref[...], nkv, preferred_element_type=F32)
    kpe = _rope_rows(proj(IN_OFFS[2], IN_OFFS[3]), c32, s32, 1, A_ROPE)
    qscale = (A_NOPE + A_ROPE) ** -0.5 * LOG2E
    zpad = jnp.zeros((A_QK_PAD - A_NOPE - A_ROPE, ts), F32)
    dqk = A_NOPE + A_ROPE
    for h in range(A_HEADS):
        qh = qa[h * dqk:(h + 1) * dqk]
        q_pe = _rope_rows(qh[A_NOPE:], c32, s32, 1, A_ROPE)
        qa_ref[0, h, 0] = (jnp.concatenate([qh[:A_NOPE], q_pe, zpad], axis=0) * qscale).astype(BF16)
        kvh = kva[h * (A_NOPE + A_V):(h + 1) * (A_NOPE + A_V)]
        kT = jnp.concatenate([kvh[:A_NOPE], kpe, zpad], axis=0)
        ka_ref[0, h] = kT.T.astype(BF16)
        va_ref[0, h, 0] = kvh[A_NOPE:].astype(BF16)

    pb = proj(IN_OFFS[3], IN_OFFS[6])
    qd = _rope_rows(pb[0:256], c32, s32, 2 * B_HEADS, B_D) * (B_D ** -0.5 * LOG2E)
    qd_ref[0, 0] = qd.astype(BF16)
    kd = _rope_rows(pb[256:512], c32, s32, 2 * B_HEADS, B_D)
    kd_ref[0] = kd.T.astype(BF16)
    for h in range(B_HEADS):
        vd_ref[0, h, 0] = pb[512 + h * B_V:512 + (h + 1) * B_V].astype(BF16)

    pc = proj(IN_OFFS[6], IN_OFFS[9])
    qc = _rope_rows(pc[0:256], c64, s64, C_HEADS, C_DIM) * (C_DIM ** -0.5 * LOG2E)
    kc = _rope_rows(pc[256:512], c64, s64, C_HEADS, C_DIM)
    qc_ref[0] = qc.T
    kc_ref[0] = kc.T
    vc_ref[0] = pc[512:768].T


def _inproj_layer(xT, winT, gq, gkv, wuqT, wukvT, rope):
    bsz, _, seq = xT.shape
    ts = SEQ_TILE
    nt = seq // ts
    c32, s32, c64, s64 = rope
    full = lambda shape: pl.BlockSpec(shape, lambda bi, i: (0,) * len(shape))
    tab = lambda rows: pl.BlockSpec((rows, ts), lambda bi, i: (0, i))
    out_shape = (
        jax.ShapeDtypeStruct((bsz, A_HEADS, nt, A_QK_PAD, ts), BF16),
        jax.ShapeDtypeStruct((bsz, A_HEADS, seq, A_QK_PAD), BF16),
        jax.ShapeDtypeStruct((bsz, A_HEADS, nt, A_V, ts), BF16),
        jax.ShapeDtypeStruct((bsz, nt, 2 * B_HEADS * B_D, ts), BF16),
        jax.ShapeDtypeStruct((bsz, seq, 2 * B_HEADS * B_D), BF16),
        jax.ShapeDtypeStruct((bsz, B_HEADS, nt, B_V, ts), BF16),
        jax.ShapeDtypeStruct((bsz, seq, C_HEADS * C_DIM), F32),
        jax.ShapeDtypeStruct((bsz, seq, C_HEADS * C_DIM), F32),
        jax.ShapeDtypeStruct((bsz, seq, C_HEADS * C_DIM), F32),
    )
    out_specs = (
        pl.BlockSpec((1, A_HEADS, 1, A_QK_PAD, ts), lambda bi, i: (bi, 0, i, 0, 0)),
        pl.BlockSpec((1, A_HEADS, ts, A_QK_PAD), lambda bi, i: (bi, 0, i, 0)),
        pl.BlockSpec((1, A_HEADS, 1, A_V, ts), lambda bi, i: (bi, 0, i, 0, 0)),
        pl.BlockSpec((1, 1, 256, ts), lambda bi, i: (bi, i, 0, 0)),
        pl.BlockSpec((1, ts, 256), lambda bi, i: (bi, i, 0)),
        pl.BlockSpec((1, B_HEADS, 1, B_V, ts), lambda bi, i: (bi, 0, i, 0, 0)),
        pl.BlockSpec((1, ts, 256), lambda bi, i: (bi, i, 0)),
        pl.BlockSpec((1, ts, 256), lambda bi, i: (bi, i, 0)),
        pl.BlockSpec((1, ts, 256), lambda bi, i: (bi, i, 0)),
    )
    return pl.pallas_call(
        _inproj_kernel,
        grid=(bsz, nt),
        in_specs=[
            pl.BlockSpec((1, D_MODEL, ts), lambda bi, i: (bi, 0, i)),
            full(winT.shape), full(gq.shape), full(gkv.shape), full(wuqT.shape), full(wukvT.shape),
            tab(A_ROPE // 2), tab(A_ROPE // 2), tab(C_DIM // 2), tab(C_DIM // 2),
        ],
        out_specs=out_specs,
        out_shape=out_shape,
        compiler_params=_cparams(("parallel", "parallel")),
        name="inproj",
    )(xT, winT, gq, gkv, wuqT, wukvT, c32, s32, c64, s64)


def _attn_kernel(q_ref, k_ref, v_ref, o_ref, s_ref, p_ref, *, tk, nk, group_rows):
    chains, tc = s_ref.shape[1], s_ref.shape[3]
    qv = q_ref.at[0] if len(q_ref.shape) == 4 else q_ref.at[0, 0]
    kv = k_ref.at[0] if len(k_ref.shape) == 3 else k_ref.at[0, 0]
    ts = qv.shape[-1]
    assert ts % tc == 0 and (chains * tc) % ts == 0

    def query_tile(qi, _):
        args = (qi, qv, kv, v_ref, o_ref, s_ref, p_ref)
        jump = _attn_query_tile_fast(*args, tk=tk, nk=nk, group_rows=group_rows)

        @pl.when(jump > MAX_SHIFT_LAG)
        def _():
            _attn_query_tile(*args, tk=tk, nk=nk, group_rows=group_rows)
        return 0

    lax.fori_loop(0, (qv.shape[0] * ts) // (chains * tc), query_tile, 0)


def _load_queries(qi, qv, chains, tc, group_rows):
    ts = qv.shape[-1]
    qs = []
    for c in range(chains):
        q = qv[qi * (chains * tc // ts) + (c * tc) // ts, :, (c * tc) % ts:(c * tc) % ts + tc]
        if group_rows:
            rows = lax.broadcasted_iota(jnp.int32, q.shape, 0)
            lo = pl.program_id(1) * group_rows
            q = jnp.where((rows >= lo) & (rows < lo + group_rows), q, jnp.zeros_like(q))
        qs.append(q)
    return qs


def _attn_query_tile_fast(qi, qv, kv, v_ref, o_ref, s_ref, p_ref, *, tk, nk, group_rows):
    chains, tc = s_ref.shape[1], s_ref.shape[3]
    qs = _load_queries(qi, qv, chains, tc, group_rows)
    dv = v_ref.shape[-2]
    ones_rows = (lax.broadcasted_iota(jnp.int32, (SUM_ROWS, tk), 0) == 0).astype(BF16)
    vsub = v_ref.shape[-1] // tk

    def scores(i):
        kc = kv[pl.ds(pl.multiple_of(i * tk, tk), tk), :]
        return [jnp.dot(kc, qs[c], preferred_element_type=F32) for c in range(chains)]

    def value_matmul(i, slot):
        blk = i // vsub if isinstance(i, int) else lax.div(i, vsub)
        part = slot % vsub
        vc = jnp.concatenate([v_ref[0, 0, blk, :, part * tk:(part + 1) * tk], ones_rows], axis=0)
        return [jnp.dot(vc, p_ref[slot, c], preferred_element_type=F32) for c in range(chains)]

    def step(i, carry, slot):
        ss = scores(i)
        pvs = value_matmul(i - 1, 1 - slot)
        out = []
        for c, (shift, acc, a1, a2, lag) in enumerate(carry):
            cmax = jnp.max(ss[c], axis=0, keepdims=True)
            p_ref[slot, c] = jnp.exp2(ss[c] - shift).astype(BF16)
            new_shift = jnp.maximum(shift, cmax)
            out.append((new_shift, a1 * acc + pvs[c], a2, jnp.exp2(shift - new_shift),
                        jnp.maximum(lag, cmax - shift)))
        return tuple(out)

    carry = []
    for c, s in enumerate(scores(0)):
        shift = jnp.max(s, axis=0, keepdims=True)
        p_ref[0, c] = jnp.exp2(s - shift).astype(BF16)
        one = jnp.ones((1, tc), F32)
        carry.append((shift, jnp.zeros((dv + SUM_ROWS, tc), F32), one, one, jnp.zeros((1, tc), F32)))
    assert ATTN_UNROLL % PIPE_SLOTS == 0 and PIPE_SLOTS == 2 and nk % 2 == 0
    lead = 2 + (nk - 2) % ATTN_UNROLL
    carry = tuple(carry)
    for i in range(1, lead):
        carry = step(i, carry, i % PIPE_SLOTS)

    def steps(j, carry):
        for u in range(ATTN_UNROLL):
            carry = step(ATTN_UNROLL * j + lead + u, carry, u % PIPE_SLOTS)
        return carry

    carry = lax.fori_loop(0, (nk - lead) // ATTN_UNROLL, steps, carry)
    pvs = value_matmul(nk - 1, (nk - 1) % PIPE_SLOTS)
    jump = jnp.zeros((1, tc), F32)
    for c, (_, acc, a1, _, lag) in enumerate(carry):
        acc = a1 * acc + pvs[c]
        o_ref[0, 0, qi * chains + c] = acc[:dv] / acc[dv:dv + 1]
        jump = jnp.maximum(jump, lag)
    return jnp.max(jump)


def _attn_query_tile(qi, qv, kv, v_ref, o_ref, s_ref, p_ref, *, tk, nk, group_rows):
    chains, tc = s_ref.shape[1], s_ref.shape[3]
    qs = _load_queries(qi, qv, chains, tc, group_rows)

    dv = v_ref.shape[-2]
    ones_rows = (lax.broadcasted_iota(jnp.int32, (SUM_ROWS, tk), 0) == 0).astype(BF16)

    def score_matmul(i, slot):
        start = pl.multiple_of(i * tk, tk)
        kc = kv[pl.ds(start, tk), :]
        cmax = []
        for c in range(chains):
            s = jnp.dot(kc, qs[c], preferred_element_type=F32)
            s_ref[slot, c] = s
            cmax.append(jnp.max(s, axis=0, keepdims=True))
        return tuple(cmax)

    vsub = v_ref.shape[-1] // tk
    assert PIPE_SLOTS % vsub == 0

    def value_matmul(i, slot, carry):
        part = slot % vsub
        blk = i // vsub if isinstance(i, int) else lax.div(i, vsub)
        vc = jnp.concatenate([v_ref[0, 0, blk, :, part * tk:(part + 1) * tk], ones_rows],
                             axis=0)
        pvs = [jnp.dot(vc, p_ref[slot, c], preferred_element_type=F32) for c in range(chains)]
        return tuple((m, alpha * acc + pvs[c], alpha) for c, (m, acc, alpha) in enumerate(carry))

    def step(i, carry, cmax, slot, first=False, last=False):
        nxt, prv = (slot + 1) % PIPE_SLOTS, (slot - 1) % PIPE_SLOTS
        cmax_next = None if last else score_matmul(i + 1, nxt)
        if not first:
            carry = value_matmul(i - 1, prv, carry)
        out = []
        for c, (m, acc, _) in enumerate(carry):
            m_new = jnp.maximum(m, cmax[c])
            p_ref[slot, c] = jnp.exp2(s_ref[slot, c] - m_new).astype(BF16)
            out.append((m_new, acc, jnp.exp2(m - m_new)))
        return tuple(out), cmax_next

    assert ATTN_UNROLL % PIPE_SLOTS == 0
    lead = 2 + (nk - 4) % ATTN_UNROLL

    def steps(j, state):
        carry, cmax = state
        for u in range(ATTN_UNROLL):
            carry, cmax = step(ATTN_UNROLL * j + lead + u, carry, cmax, (lead + u) % PIPE_SLOTS)
        return carry, cmax

    carry = tuple((jnp.full((1, tc), NEG_BIG, F32), jnp.zeros((dv + SUM_ROWS, tc), F32),
                   jnp.ones((1, tc), F32)) for _ in range(chains))
    cmax = score_matmul(0, 0)
    for i in range(lead):
        carry, cmax = step(i, carry, cmax, i % PIPE_SLOTS, first=(i == 0))
    carry, cmax = lax.fori_loop(0, (nk - 2 - lead) // ATTN_UNROLL, steps, (carry, cmax))
    carry, cmax = step(nk - 2, carry, cmax, (nk - 2) % PIPE_SLOTS)
    carry, _ = step(nk - 1, carry, cmax, (nk - 1) % PIPE_SLOTS, last=True)
    carry = value_matmul(nk - 1, (nk - 1) % PIPE_SLOTS, carry)
    for c, (_, acc, _) in enumerate(carry):
        o_ref[0, 0, qi * chains + c] = acc[:dv] / acc[dv:dv + 1]


def _attention(qT, k, vT, *, heads, shared_qk, v_heads):
    bsz, nt, ts = qT.shape[0], qT.shape[-3], qT.shape[-1]
    seq = nt * ts
    nv, dv, tv = vT.shape[2], vT.shape[3], vT.shape[4]
    tk, tc = ATTN_TK, ATTN_TQ // ATTN_CHAINS
    nk = seq // tk
    vrep = heads // v_heads
    if shared_qk:
        dq_all = qT.shape[2]
        q_spec = pl.BlockSpec((1, nt, dq_all, ts), lambda bi, h: (bi, 0, 0, 0))
        k_spec = pl.BlockSpec((1, seq, dq_all), lambda bi, h: (bi, 0, 0))
        group_rows = dq_all // heads
    else:
        dq = qT.shape[3]
        q_spec = pl.BlockSpec((1, 1, nt, dq, ts), lambda bi, h: (bi, h, 0, 0, 0))
        k_spec = pl.BlockSpec((1, 1, seq, dq), lambda bi, h: (bi, h, 0, 0))
        group_rows = 0
    return pl.pallas_call(
        functools.partial(_attn_kernel, tk=tk, nk=nk, group_rows=group_rows),
        scratch_shapes=[pltpu.VMEM((PIPE_SLOTS, ATTN_CHAINS, tk, tc), F32),
                        pltpu.VMEM((PIPE_SLOTS, ATTN_CHAINS, tk, tc), BF16)],
        grid=(bsz, heads),
        in_specs=[
            q_spec, k_spec,
            pl.BlockSpec((1, 1, nv, dv, tv), lambda bi, h: (bi, h // vrep, 0, 0, 0)),
        ],
        out_specs=pl.BlockSpec((1, 1, seq // tc, dv, tc), lambda bi, h: (bi, h, 0, 0, 0)),
        out_shape=jax.ShapeDtypeStruct((bsz, heads, seq // tc, dv, tc), F32),
        compiler_params=_cparams(("parallel", "parallel")),
        name="attn_shared" if shared_qk else "attn",
    )(qT, k, vT)


def _dilated_kernel(q_ref, k_ref, v_ref, o_ref, m_ref, l_ref, acc_ref, *, seq, sup):
    sb = pl.program_id(2)
    p0 = sb * sup
    m_ref[...] = jnp.full(m_ref.shape, NEG_BIG, F32)
    l_ref[...] = jnp.zeros(l_ref.shape, F32)
    acc_ref[...] = jnp.zeros(acc_ref.shape, F32)
    pair = 2 * C_DIM
    lane = lax.broadcasted_iota(jnp.int32, (1, pair), 1)
    h0 = lane < C_DIM
    qi = lax.broadcasted_iota(jnp.int32, (DIL_QBLK, 1), 0)
    ki = lax.broadcasted_iota(jnp.int32, (1, DIL_KBLK), 1)
    ones0 = jnp.broadcast_to(h0.astype(F32), (DIL_KBLK, pair))
    ones1 = 1.0 - ones0

    for _, dil in C_BRANCHES:
        cls_len = seq // dil
        blocks = sup // (DIL_QBLK * dil)

        def body(tt, carry, dil=dil, cls_len=cls_len, blocks=blocks):
            blk = []
            for u in range(DIL_UNROLL):
                t = tt * DIL_UNROLL + u
                r = lax.div(t, blocks)
                jb = lax.rem(t, blocks)
                c0 = sb * (sup // dil) + DIL_QBLK * jb
                ks = jnp.clip(c0 - C_HALF, 0, cls_len - DIL_KBLK)
                qrow = r + dil * DIL_QBLK * jb
                ssl = pl.ds(qrow, DIL_QBLK, stride=dil)
                ksl = pl.ds(r + dil * ks, DIL_KBLK, stride=dil)
                kf = k_ref[0, ksl, :]
                vf = v_ref[0, ksl, :]
                kcat = jnp.concatenate([jnp.where(h0, kf, 0.0), jnp.where(h0, 0.0, kf)], axis=0)
                vaug = jnp.concatenate(
                    [jnp.concatenate([jnp.where(h0, vf, 0.0), ones0], axis=1),
                     jnp.concatenate([jnp.where(h0, 0.0, vf), ones1], axis=1)], axis=0)
                blk.append(dict(
                    ssl=ssl,
                    q=q_ref[0, pl.ds(p0 + qrow, DIL_QBLK, stride=dil), :].astype(BF16),
                    kcat=kcat.astype(BF16),
                    vaug=vaug.astype(BF16),
                    band=jnp.abs((c0 + qi) - (ks + ki)) <= C_HALF,
                    acc=acc_ref[ssl, :], m=m_ref[ssl, :], l=l_ref[ssl, :]))
            for b in blk:
                b["s"] = lax.dot_general(b["q"], b["kcat"], (((1,), (1,)), ((), ())),
                                         preferred_element_type=F32)
            for b in blk:
                s0 = jnp.where(b["band"], b["s"][:, :DIL_KBLK], NEG_BIG)
                s1 = jnp.where(b["band"], b["s"][:, DIL_KBLK:], NEG_BIG)
                t0 = jnp.maximum(s0[:, :pair], s0[:, pair:])
                t1 = jnp.maximum(s1[:, :pair], s1[:, pair:])
                m0 = jnp.max(jnp.where(h0, jnp.maximum(t0, b["m"]), t0), axis=1, keepdims=True)
                m1 = jnp.max(jnp.where(h0, t1, jnp.maximum(t1, b["m"])), axis=1, keepdims=True)
                m_new = jnp.where(h0, m0, m1)
                b["p"] = jnp.concatenate([jnp.exp2(s0 - m0), jnp.exp2(s1 - m1)],
                                         axis=1).astype(BF16)
                b["alpha"] = jnp.exp2(b["m"] - m_new)
                b["m"] = m_new
            for b in blk:
                pv = jnp.dot(b["p"], b["vaug"], preferred_element_type=F32)
                b["acc"] = b["alpha"] * b["acc"] + pv[:, :pair]
                b["l"] = b["alpha"] * b["l"] + pv[:, pair:]
            for b in blk:
                acc_ref[b["ssl"], :] = b["acc"]
                m_ref[b["ssl"], :] = b["m"]
                l_ref[b["ssl"], :] = b["l"]
            return carry

        lax.fori_loop(0, dil * blocks // DIL_UNROLL, body, 0)

    o_ref[0] = acc_ref[...] / l_ref[...]


def _dilated_attention(qc, kc, vc):
    bsz, seq, width = qc.shape
    sup = min(DIL_SUPER, seq)
    pair = 2 * C_DIM
    in_spec = pl.BlockSpec((1, seq, pair), lambda bi, hp, sb: (bi, 0, hp))
    return pl.pallas_call(
        functools.partial(_dilated_kernel, seq=seq, sup=sup),
        grid=(bsz, width // pair, seq // sup),
        in_specs=[in_spec, in_spec, in_spec],
        out_specs=pl.BlockSpec((1, sup, pair), lambda bi, hp, sb: (bi, sb, hp)),
        out_shape=jax.ShapeDtypeStruct((bsz, seq, width), F32),
        scratch_shapes=[pltpu.VMEM((sup, pair), F32)] * 3,
        compiler_params=_cparams(("parallel", "parallel", "arbitrary")),
        name="dilated",
    )(qc, kc, vc)


def _outproj_kernel(x_ref, oa_ref, od_ref, oc_ref, woa_ref, wob_ref, woc_ref, lam_ref, dg_ref,
                    g_ref, b_ref, o_ref, y_ref, *, lambda_init):
    @pl.when(pl.program_id(0) == 0)
    def _():
        y_ref[...] = jnp.zeros(y_ref.shape, F32)

    ts = x_ref.shape[-1]
    lf = lam_ref[...]
    lam = (jnp.exp(jnp.sum(lf[0:1] * lf[1:2], axis=1, keepdims=True))
           - jnp.exp(jnp.sum(lf[2:3] * lf[3:4], axis=1, keepdims=True)) + lambda_init)
    def tile(ref):
        return jnp.concatenate([ref[0, :, c] for c in range(ref.shape[2])], axis=-1)

    od = tile(od_ref).reshape(B_HEADS, 2, B_V, ts)
    ob = od[:, 0] - lam * od[:, 1]
    ms = jnp.mean(ob * ob, axis=1, keepdims=True)
    ob = ob * lax.rsqrt(ms + NORM_EPS) * dg_ref[...] * (1.0 - lambda_init)
    ob = ob.reshape(B_HEADS * B_V, ts).astype(BF16)
    oa = tile(oa_ref).reshape(A_HEADS * A_V, ts).astype(BF16)
    y = jnp.dot(woa_ref[...], oa, preferred_element_type=F32)
    y += jnp.dot(wob_ref[...], ob, preferred_element_type=F32)
    y += lax.dot_general(woc_ref[...], oc_ref[0].astype(BF16), (((1,), (1,)), ((), ())),
                         preferred_element_type=F32)
    o_ref[0] = _layer_norm_rows(y_ref[...], g_ref[...], b_ref[...])
    y_ref[...] = ALPHA * x_ref[0] + y


def _outproj_layer(xT, oaT, odT, oc, woaT, wobT, wocT, lam, dg, g, b, lambda_init):
    bsz, _, seq = xT.shape
    ts = SEQ_TILE
    nt = seq // ts
    tiles = bsz * nt
    full = lambda a: pl.BlockSpec(a.shape, lambda t: (0,) * a.ndim)
    heads, _, dv, tc = oaT.shape[1:]

    def cur(t):
        tt = jnp.minimum(t, tiles - 1)
        return tt // nt, tt % nt

    def col(t):
        bi, i = cur(t)
        return (bi, 0, i)

    def chunk(t):
        bi, i = cur(t)
        return (bi, 0, i, 0, 0)

    def tokens(t):
        bi, i = cur(t)
        return (bi, i, 0)

    def prev(t):
        tt = jnp.maximum(t - 1, 0)
        return (tt // nt, 0, tt % nt)

    chunks = pl.BlockSpec((1, heads, ts // tc, dv, tc), chunk)
    return pl.pallas_call(
        functools.partial(_outproj_kernel, lambda_init=lambda_init),
        grid=(tiles + 1,),
        in_specs=[
            pl.BlockSpec((1, D_MODEL, ts), col), chunks, chunks,
            pl.BlockSpec((1, ts, C_HEADS * C_DIM), tokens),
            full(woaT), full(wobT), full(wocT), full(lam), full(dg), full(g), full(b),
        ],
        out_specs=pl.BlockSpec((1, D_MODEL, ts), prev),
        out_shape=jax.ShapeDtypeStruct(xT.shape, F32),
        scratch_shapes=[pltpu.VMEM((D_MODEL, ts), F32)],
        compiler_params=_cparams(("arbitrary",)),
        name="outproj",
    )(xT, oaT, odT, oc, woaT, wobT, wocT, lam, dg, g, b)


def _rope_tables(seq, dim):
    inv = 1.0 / (ROPE_THETA ** (jnp.arange(0, dim, 2, dtype=F32) / dim))
    ang = inv[:, None] * jnp.arange(seq, dtype=F32)[None, :]
    return jnp.cos(ang), jnp.sin(ang)


def _prep_layer(i, ln_g, ln_b, ffn_w_gate, ffn_w_up, ffn_w_down, w_in, mla_q_norm, mla_kv_norm,
                mla_w_uq, mla_w_ukv, diff_lambda, diff_subln, w_out):
    tf = D_FF // FF_CHUNKS
    col = lambda v: v.astype(F32)[:, None]
    ffn = []
    for s in range(2):
        wg = ffn_w_gate[i, s].T.reshape(FF_CHUNKS, tf, D_MODEL)
        wu = ffn_w_up[i, s].T.reshape(FF_CHUNKS, tf, D_MODEL)
        wgu = jnp.concatenate([wg, wu], axis=1).astype(BF16)
        wd = ffn_w_down[i, s].T.astype(BF16)
        ffn.append((wgu, wd))
    woT = w_out[i].T.astype(BF16)
    na, nb = A_HEADS * A_V, B_HEADS * B_V
    return dict(
        ffn=ffn,
        ln=[(col(ln_g[i, s]), col(ln_b[i, s])) for s in range(3)],
        winT=w_in[i].T.astype(BF16),
        gq=col(mla_q_norm[i]), gkv=col(mla_kv_norm[i]),
        wuqT=mla_w_uq[i].T.astype(BF16), wukvT=mla_w_ukv[i].T.astype(BF16),
        woaT=woT[:, :na], wobT=woT[:, na:na + nb], wocT=woT[:, na + nb:],
        lam=diff_lambda[i].astype(F32), dg=col(diff_subln[i]),
    )


def _run(x, layers):
    bsz, seq, _ = x.shape
    assert seq % DIL_SUPER == 0 and seq // C_BRANCHES[-1][1] >= DIL_KBLK, seq
    assert seq % SEQ_TILE == 0 and seq % ATTN_TQ == 0 and FF_CHUNKS == 2
    rope = _rope_tables(seq, A_ROPE) + _rope_tables(seq, C_DIM)
    xT = x
    for i, w in enumerate(layers):
        lambda_init = 0.8 - 0.6 * math.exp(-0.3 * i)
        xT = _ffn_layer(xT, *w["ffn"][0], *w["ln"][0], x_tokens=(i == 0))
        qaT, ka, vaT, qdT, kd, vdT, qc, kc, vc = _inproj_layer(
            xT, w["winT"], w["gq"], w["gkv"], w["wuqT"], w["wukvT"], rope)
        oaT = _attention(qaT, ka, vaT, heads=A_HEADS, shared_qk=False, v_heads=A_HEADS)
        odT = _attention(qdT, kd, vdT, heads=2 * B_HEADS, shared_qk=True, v_heads=B_HEADS)
        oc = _dilated_attention(qc, kc, vc)
        xT = _outproj_layer(xT, oaT, odT, oc, w["woaT"], w["wobT"], w["wocT"], w["lam"], w["dg"],
                            *w["ln"][1], lambda_init)
        xT = _ffn_layer(xT, *w["ffn"][1], *w["ln"][2], o_tokens=(i == len(layers) - 1))
    return xT


def kernel(x_prompt, x_sample, ln_g, ln_b, ffn_w_gate, ffn_w_up, ffn_w_down, w_in, mla_q_norm,
           mla_kv_norm, mla_w_uq, mla_w_ukv, diff_lambda, diff_subln, w_out):
    params = (ln_g, ln_b, ffn_w_gate, ffn_w_up, ffn_w_down, w_in, mla_q_norm, mla_kv_norm,
              mla_w_uq, mla_w_ukv, diff_lambda, diff_subln, w_out)
    layers = [_prep_layer(i, *params) for i in range(DEPTH)]
    return _run(x_prompt, layers), _run(x_sample, layers)
```

```python
import functools
import math

import jax
import jax.numpy as jnp
from jax import lax
from jax.experimental import pallas as pl
from jax.experimental.pallas import tpu as pltpu

F32 = jnp.float32
BF16 = jnp.bfloat16

D_MODEL = 1024
D_FF = 2816
DEPTH = 4
ROPE_THETA = 10000.0
NORM_EPS = 1e-5
A_HEADS, A_NOPE, A_ROPE, A_V = 8, 64, 32, 64
A_Q_LORA, A_KV_LORA = 384, 256
A_QK_PAD = 128
B_HEADS, B_D = 4, 32
B_V = 2 * B_D
C_HEADS, C_DIM = 4, 64
C_BRANCHES = ((128, 1), (512, 4), (2048, 16))
C_HALF = 64
IN_SPLITS = (A_Q_LORA, A_KV_LORA, A_ROPE, 256, 256, 256, 256, 256, 256)
IN_OFFS = tuple(int(sum(IN_SPLITS[:i])) for i in range(len(IN_SPLITS) + 1))
ALPHA = (2 * DEPTH) ** 0.25
LOG2E = math.log2(math.e)

VMEM_LIMIT_BYTES = 56 * 1024 * 1024
SEQ_TILE = 512
FF_CHUNKS = 2
ATTN_TQ = 2048
ATTN_TK = 512
ATTN_CHAINS = 8
ATTN_UNROLL = 6
PIPE_SLOTS = 2
SUM_ROWS = 16
DIL_QBLK = 128
DIL_KBLK = 256
DIL_SUPER = 2048
DIL_UNROLL = 4
NEG_BIG = -1e30
MAX_SHIFT_LAG = 64.0


def _cparams(sem):
    return pltpu.CompilerParams(dimension_semantics=sem, vmem_limit_bytes=VMEM_LIMIT_BYTES)


def _layer_norm_rows(y, g, b):
    mu = jnp.mean(y, axis=0, keepdims=True)
    d = y - mu
    var = jnp.mean(d * d, axis=0, keepdims=True)
    return d * lax.rsqrt(var + NORM_EPS) * g + b


def _rms_norm_rows(y, g):
    ms = jnp.mean(y * y, axis=0, keepdims=True)
    return y * lax.rsqrt(ms + NORM_EPS) * g


def _rope_rows(x, c, s, groups, dim):
    half = dim // 2
    t = x.shape[-1]
    xg = x.reshape(groups, 2, half, t)
    x1, x2 = xg[:, 0], xg[:, 1]
    o1 = x1 * c - x2 * s
    o2 = x2 * c + x1 * s
    return jnp.stack([o1, o2], axis=1).reshape(groups * dim, t)


def _ffn_kernel(x_ref, wgu_ref, wd_ref, g_ref, b_ref, o_ref, y_ref, *xt_ref, tf, o_tokens):
    @pl.when(pl.program_id(0) == 0)
    def _():
        y_ref[...] = jnp.zeros(y_ref.shape, F32)

    if xt_ref:
        xt_ref[0][...] = x_ref[0].T
        x_rows = xt_ref[0]
    else:
        x_rows = x_ref.at[0]
    xb = x_rows[...].astype(BF16)
    gus = [jnp.dot(wgu_ref[j], xb, preferred_element_type=F32) for j in range(FF_CHUNKS)]
    h = jnp.concatenate([(gu[:tf] * jax.nn.sigmoid(gu[:tf]) * gu[tf:]).astype(BF16)
                         for gu in gus], axis=0)
    normed = _layer_norm_rows(y_ref[...], g_ref[...], b_ref[...])
    o_ref[0] = normed.T if o_tokens else normed
    half = D_MODEL // 2
    for r in range(0, D_MODEL, half):
        ff = jnp.dot(wd_ref[r:r + half, :], h, preferred_element_type=F32)
        y_ref[r:r + half, :] = ALPHA * x_rows[r:r + half, :] + 0.5 * ff


def _ffn_layer(x, wguT, wdT, g, b, *, x_tokens=False, o_tokens=False):
    bsz = x.shape[0]
    seq = x.shape[1] if x_tokens else x.shape[2]
    ts = SEQ_TILE
    tf = D_FF // FF_CHUNKS
    nt = seq // ts
    tiles = bsz * nt

    def tile_spec(tokens, clamp):
        def index(t):
            tt = clamp(t)
            return (tt // nt, tt % nt, 0) if tokens else (tt // nt, 0, tt % nt)
        return pl.BlockSpec((1, ts, D_MODEL) if tokens else (1, D_MODEL, ts), index)

    resident = lambda a: pl.BlockSpec(a.shape, lambda t: (0,) * a.ndim,
                                      pipeline_mode=pl.Buffered(1))
    tile = pltpu.VMEM((D_MODEL, ts), F32)
    return pl.pallas_call(
        functools.partial(_ffn_kernel, tf=tf, o_tokens=o_tokens),
        grid=(tiles + 1,),
        in_specs=[tile_spec(x_tokens, lambda t: jnp.minimum(t, tiles - 1)),
                  resident(wguT), resident(wdT), resident(g), resident(b)],
        out_specs=tile_spec(o_tokens, lambda t: jnp.maximum(t - 1, 0)),
        out_shape=jax.ShapeDtypeStruct(
            (bsz, seq, D_MODEL) if o_tokens else (bsz, D_MODEL, seq), F32),
        scratch_shapes=[tile, tile] if x_tokens else [tile],
        compiler_params=_cparams(("arbitrary",)),
        name="ffn",
    )(x, wguT, wdT, g, b)


def _inproj_kernel(x_ref, win_ref, gq_ref, gkv_ref, wuq_ref, wukv_ref,
                   c32_ref, s32_ref, c64_ref, s64_ref,
                   qa_ref, ka_ref, va_ref, qd_ref, kd_ref, vd_ref, qc_ref, kc_ref, vc_ref):
    xb = x_ref[0].astype(BF16)
    ts = xb.shape[-1]
    c32, s32 = c32_ref[...], s32_ref[...]
    c64, s64 = c64_ref[...], s64_ref[...]

    def proj(lo, hi):
        return jnp.dot(win_ref[lo:hi, :], xb, preferred_element_type=F32)

    nq = _rms_norm_rows(proj(IN_OFFS[0], IN_OFFS[1]), gq_ref[...]).astype(BF16)
    qa = jnp.dot(wuq_ref[...], nq, preferred_element_type=F32)
    nkv = _rms_norm_rows(proj(IN_OFFS[1], IN_OFFS[2]), gkv_ref[...]).astype(BF16)
    kva = jnp.dot(wukv_ref[...], nkv, preferred_element_type=F32)
    kpe = _rope_rows(proj(IN_OFFS[2], IN_OFFS[3]), c32, s32, 1, A_ROPE)
    qscale = (A_NOPE + A_ROPE) ** -0.5 * LOG2E
    zpad = jnp.zeros((A_QK_PAD - A_NOPE - A_ROPE, ts), F32)
    dqk = A_NOPE + A_ROPE
    for h in range(A_HEADS):
        qh = qa[h * dqk:(h + 1) * dqk]
        q_pe = _rope_rows(qh[A_NOPE:], c32, s32, 1, A_ROPE)
        qa_ref[0, h, 0] = (jnp.concatenate([qh[:A_NOPE], q_pe, zpad], axis=0) * qscale).astype(BF16)
        kvh = kva[h * (A_NOPE + A_V):(h + 1) * (A_NOPE + A_V)]
        kT = jnp.concatenate([kvh[:A_NOPE], kpe, zpad], axis=0)
        ka_ref[0, h] = kT.T.astype(BF16)
        va_ref[0, h, 0] = kvh[A_NOPE:].astype(BF16)

    pb = proj(IN_OFFS[3], IN_OFFS[6])
    qd = _rope_rows(pb[0:256], c32, s32, 2 * B_HEADS, B_D) * (B_D ** -0.5 * LOG2E)
    qd_ref[0, 0] = qd.astype(BF16)
    kd = _rope_rows(pb[256:512], c32, s32, 2 * B_HEADS, B_D)
    kd_ref[0] = kd.T.astype(BF16)
    for h in range(B_HEADS):
        vd_ref[0, h, 0] = pb[512 + h * B_V:512 + (h + 1) * B_V].astype(BF16)

    pc = proj(IN_OFFS[6], IN_OFFS[9])
    qc = _rope_rows(pc[0:256], c64, s64, C_HEADS, C_DIM) * (C_DIM ** -0.5 * LOG2E)
    kc = _rope_rows(pc[256:512], c64, s64, C_HEADS, C_DIM)
    qc_ref[0] = qc.T
    kc_ref[0] = kc.T
    vc_ref[0] = pc[512:768].T


def _inproj_layer(xT, winT, gq, gkv, wuqT, wukvT, rope):
    bsz, _, seq = xT.shape
    ts = SEQ_TILE
    nt = seq // ts
    c32, s32, c64, s64 = rope
    full = lambda shape: pl.BlockSpec(shape, lambda bi, i: (0,) * len(shape))
    tab = lambda rows: pl.BlockSpec((rows, ts), lambda bi, i: (0, i))
    out_shape = (
        jax.ShapeDtypeStruct((bsz, A_HEADS, nt, A_QK_PAD, ts), BF16),
        jax.ShapeDtypeStruct((bsz, A_HEADS, seq, A_QK_PAD), BF16),
        jax.ShapeDtypeStruct((bsz, A_HEADS, nt, A_V, ts), BF16),
        jax.ShapeDtypeStruct((bsz, nt, 2 * B_HEADS * B_D, ts), BF16),
        jax.ShapeDtypeStruct((bsz, seq, 2 * B_HEADS * B_D), BF16),
        jax.ShapeDtypeStruct((bsz, B_HEADS, nt, B_V, ts), BF16),
        jax.ShapeDtypeStruct((bsz, seq, C_HEADS * C_DIM), F32),
        jax.ShapeDtypeStruct((bsz, seq, C_HEADS * C_DIM), F32),
        jax.ShapeDtypeStruct((bsz, seq, C_HEADS * C_DIM), F32),
    )
    out_specs = (
        pl.BlockSpec((1, A_HEADS, 1, A_QK_PAD, ts), lambda bi, i: (bi, 0, i, 0, 0)),
        pl.BlockSpec((1, A_HEADS, ts, A_QK_PAD), lambda bi, i: (bi, 0, i, 0)),
        pl.BlockSpec((1, A_HEADS, 1, A_V, ts), lambda bi, i: (bi, 0, i, 0, 0)),
        pl.BlockSpec((1, 1, 256, ts), lambda bi, i: (bi, i, 0, 0)),
        pl.BlockSpec((1, ts, 256), lambda bi, i: (bi, i, 0)),
        pl.BlockSpec((1, B_HEADS, 1, B_V, ts), lambda bi, i: (bi, 0, i, 0, 0)),
        pl.BlockSpec((1, ts, 256), lambda bi, i: (bi, i, 0)),
        pl.BlockSpec((1, ts, 256), lambda bi, i: (bi, i, 0)),
        pl.BlockSpec((1, ts, 256), lambda bi, i: (bi, i, 0)),
    )
    return pl.pallas_call(
        _inproj_kernel,
        grid=(bsz, nt),
        in_specs=[
            pl.BlockSpec((1, D_MODEL, ts), lambda bi, i: (bi, 0, i)),
            full(winT.shape), full(gq.shape), full(gkv.shape), full(wuqT.shape), full(wukvT.shape),
            tab(A_ROPE // 2), tab(A_ROPE // 2), tab(C_DIM // 2), tab(C_DIM // 2),
        ],
        out_specs=out_specs,
        out_shape=out_shape,
        compiler_params=_cparams(("parallel", "parallel")),
        name="inproj",
    )(xT, winT, gq, gkv, wuqT, wukvT, c32, s32, c64, s64)


def _attn_kernel(q_ref, k_ref, v_ref, o_ref, s_ref, p_ref, *, tk, nk, group_rows):
    chains, tc = s_ref.shape[1], s_ref.shape[3]
    qv = q_ref.at[0] if len(q_ref.shape) == 4 else q_ref.at[0, 0]
    kv = k_ref.at[0] if len(k_ref.shape) == 3 else k_ref.at[0, 0]
    ts = qv.shape[-1]
    assert ts % tc == 0 and (chains * tc) % ts == 0

    def query_tile(qi, _):
        args = (qi, qv, kv, v_ref, o_ref, s_ref, p_ref)
        jump = _attn_query_tile_fast(*args, tk=tk, nk=nk, group_rows=group_rows)

        @pl.when(jump > MAX_SHIFT_LAG)
        def _():
            _attn_query_tile(*args, tk=tk, nk=nk, group_rows=group_rows)
        return 0

    lax.fori_loop(0, (qv.shape[0] * ts) // (chains * tc), query_tile, 0)


def _load_queries(qi, qv, chains, tc, group_rows):
    ts = qv.shape[-1]
    qs = []
    for c in range(chains):
        q = qv[qi * (chains * tc // ts) + (c * tc) // ts, :, (c * tc) % ts:(c * tc) % ts + tc]
        if group_rows:
            rows = lax.broadcasted_iota(jnp.int32, q.shape, 0)
            lo = pl.program_id(1) * group_rows
            q = jnp.where((rows >= lo) & (rows < lo + group_rows), q, jnp.zeros_like(q))
        qs.append(q)
    return qs


def _attn_query_tile_fast(qi, qv, kv, v_ref, o_ref, s_ref, p_ref, *, tk, nk, group_rows):
    chains, tc = s_ref.shape[1], s_ref.shape[3]
    qs = _load_queries(qi, qv, chains, tc, group_rows)
    dv = v_ref.shape[-2]
    ones_rows = (lax.broadcasted_iota(jnp.int32, (SUM_ROWS, tk), 0) == 0).astype(BF16)
    vsub = v_ref.shape[-1] // tk

    def scores(i):
        kc = kv[pl.ds(pl.multiple_of(i * tk, tk), tk), :]
        return [jnp.dot(kc, qs[c], preferred_element_type=F32) for c in range(chains)]

    def value_matmul(i, slot):
        blk = i // vsub if isinstance(i, int) else lax.div(i, vsub)
        part = slot % vsub
        vc = jnp.concatenate([v_ref[0, 0, blk, :, part * tk:(part + 1) * tk], ones_rows], axis=0)
        return [jnp.dot(vc, p_ref[slot, c], preferred_element_type=F32) for c in range(chains)]

    def step(i, carry, slot):
        ss = scores(i)
        pvs = value_matmul(i - 1, 1 - slot)
        out = []
        for c, (shift, acc, a1, a2, lag) in enumerate(carry):
            cmax = jnp.max(ss[c], axis=0, keepdims=True)
            p_ref[slot, c] = jnp.exp2(ss[c] - shift).astype(BF16)
            new_shift = jnp.maximum(shift, cmax)
            out.append((new_shift, a1 * acc + pvs[c], a2, jnp.exp2(shift - new_shift),
                        jnp.maximum(lag, cmax - shift)))
        return tuple(out)

    carry = []
    for c, s in enumerate(scores(0)):
        shift = jnp.max(s, axis=0, keepdims=True)
        p_ref[0, c] = jnp.exp2(s - shift).astype(BF16)
        one = jnp.ones((1, tc), F32)
        carry.append((shift, jnp.zeros((dv + SUM_ROWS, tc), F32), one, one, jnp.zeros((1, tc), F32)))
    assert ATTN_UNROLL % PIPE_SLOTS == 0 and PIPE_SLOTS == 2 and nk % 2 == 0
    lead = 2 + (nk - 2) % ATTN_UNROLL
    carry = tuple(carry)
    for i in range(1, lead):
        carry = step(i, carry, i % PIPE_SLOTS)

    def steps(j, carry):
        for u in range(ATTN_UNROLL):
            carry = step(ATTN_UNROLL * j + lead + u, carry, u % PIPE_SLOTS)
        return carry

    carry = lax.fori_loop(0, (nk - lead) // ATTN_UNROLL, steps, carry)
    pvs = value_matmul(nk - 1, (nk - 1) % PIPE_SLOTS)
    jump = jnp.zeros((1, tc), F32)
    for c, (_, acc, a1, _, lag) in enumerate(carry):
        acc = a1 * acc + pvs[c]
        o_ref[0, 0, qi * chains + c] = (acc[:dv] / acc[dv:dv + 1]).astype(o_ref.dtype)
        jump = jnp.maximum(jump, lag)
    return jnp.max(jump)


def _attn_query_tile(qi, qv, kv, v_ref, o_ref, s_ref, p_ref, *, tk, nk, group_rows):
    chains, tc = s_ref.shape[1], s_ref.shape[3]
    qs = _load_queries(qi, qv, chains, tc, group_rows)

    dv = v_ref.shape[-2]
    ones_rows = (lax.broadcasted_iota(jnp.int32, (SUM_ROWS, tk), 0) == 0).astype(BF16)

    def score_matmul(i, slot):
        start = pl.multiple_of(i * tk, tk)
        kc = kv[pl.ds(start, tk), :]
        cmax = []
        for c in range(chains):
            s = jnp.dot(kc, qs[c], preferred_element_type=F32)
            s_ref[slot, c] = s
            cmax.append(jnp.max(s, axis=0, keepdims=True))
        return tuple(cmax)

    vsub = v_ref.shape[-1] // tk
    assert PIPE_SLOTS % vsub == 0

    def value_matmul(i, slot, carry):
        part = slot % vsub
        blk = i // vsub if isinstance(i, int) else lax.div(i, vsub)
        vc = jnp.concatenate([v_ref[0, 0, blk, :, part * tk:(part + 1) * tk], ones_rows],
                             axis=0)
        pvs = [jnp.dot(vc, p_ref[slot, c], preferred_element_type=F32) for c in range(chains)]
        return tuple((m, alpha * acc + pvs[c], alpha) for c, (m, acc, alpha) in enumerate(carry))

    def step(i, carry, cmax, slot, first=False, last=False):
        nxt, prv = (slot + 1) % PIPE_SLOTS, (slot - 1) % PIPE_SLOTS
        cmax_next = None if last else score_matmul(i + 1, nxt)
        if not first:
            carry = value_matmul(i - 1, prv, carry)
        out = []
        for c, (m, acc, _) in enumerate(carry):
            m_new = jnp.maximum(m, cmax[c])
            p_ref[slot, c] = jnp.exp2(s_ref[slot, c] - m_new).astype(BF16)
            out.append((m_new, acc, jnp.exp2(m - m_new)))
        return tuple(out), cmax_next

    assert ATTN_UNROLL % PIPE_SLOTS == 0
    lead = 2 + (nk - 4) % ATTN_UNROLL

    def steps(j, state):
        carry, cmax = state
        for u in range(ATTN_UNROLL):
            carry, cmax = step(ATTN_UNROLL * j + lead + u, carry, cmax, (lead + u) % PIPE_SLOTS)
        return carry, cmax

    carry = tuple((jnp.full((1, tc), NEG_BIG, F32), jnp.zeros((dv + SUM_ROWS, tc), F32),
                   jnp.ones((1, tc), F32)) for _ in range(chains))
    cmax = score_matmul(0, 0)
    for i in range(lead):
        carry, cmax = step(i, carry, cmax, i % PIPE_SLOTS, first=(i == 0))
    carry, cmax = lax.fori_loop(0, (nk - 2 - lead) // ATTN_UNROLL, steps, (carry, cmax))
    carry, cmax = step(nk - 2, carry, cmax, (nk - 2) % PIPE_SLOTS)
    carry, _ = step(nk - 1, carry, cmax, (nk - 1) % PIPE_SLOTS, last=True)
    carry = value_matmul(nk - 1, (nk - 1) % PIPE_SLOTS, carry)
    for c, (_, acc, _) in enumerate(carry):
        o_ref[0, 0, qi * chains + c] = (acc[:dv] / acc[dv:dv + 1]).astype(o_ref.dtype)


def _attention(qT, k, vT, *, heads, shared_qk, v_heads):
    bsz, nt, ts = qT.shape[0], qT.shape[-3], qT.shape[-1]
    seq = nt * ts
    nv, dv, tv = vT.shape[2], vT.shape[3], vT.shape[4]
    tk, tc = ATTN_TK, ATTN_TQ // ATTN_CHAINS
    nk = seq // tk
    vrep = heads // v_heads
    if shared_qk:
        dq_all = qT.shape[2]
        q_spec = pl.BlockSpec((1, nt, dq_all, ts), lambda bi, h: (bi, 0, 0, 0))
        k_spec = pl.BlockSpec((1, seq, dq_all), lambda bi, h: (bi, 0, 0))
        group_rows = dq_all // heads
    else:
        dq = qT.shape[3]
        q_spec = pl.BlockSpec((1, 1, nt, dq, ts), lambda bi, h: (bi, h, 0, 0, 0))
        k_spec = pl.BlockSpec((1, 1, seq, dq), lambda bi, h: (bi, h, 0, 0))
        group_rows = 0
    return pl.pallas_call(
        functools.partial(_attn_kernel, tk=tk, nk=nk, group_rows=group_rows),
        scratch_shapes=[pltpu.VMEM((PIPE_SLOTS, ATTN_CHAINS, tk, tc), F32),
                        pltpu.VMEM((PIPE_SLOTS, ATTN_CHAINS, tk, tc), BF16)],
        grid=(bsz, heads),
        in_specs=[
            q_spec, k_spec,
            pl.BlockSpec((1, 1, nv, dv, tv), lambda bi, h: (bi, h // vrep, 0, 0, 0)),
        ],
        out_specs=pl.BlockSpec((1, 1, seq // tc, dv, tc), lambda bi, h: (bi, h, 0, 0, 0)),
        out_shape=jax.ShapeDtypeStruct((bsz, heads, seq // tc, dv, tc), F32 if shared_qk else BF16),
        compiler_params=_cparams(("parallel", "parallel")),
        name="attn_shared" if shared_qk else "attn",
    )(qT, k, vT)


def _dilated_kernel(q_ref, k_ref, v_ref, o_ref, m_ref, l_ref, acc_ref, *, seq, sup):
    sb = pl.program_id(2)
    p0 = sb * sup
    m_ref[...] = jnp.full(m_ref.shape, NEG_BIG, F32)
    l_ref[...] = jnp.zeros(l_ref.shape, F32)
    acc_ref[...] = jnp.zeros(acc_ref.shape, F32)
    pair = 2 * C_DIM
    lane = lax.broadcasted_iota(jnp.int32, (1, pair), 1)
    h0 = lane < C_DIM
    qi = lax.broadcasted_iota(jnp.int32, (DIL_QBLK, 1), 0)
    ki = lax.broadcasted_iota(jnp.int32, (1, DIL_KBLK), 1)
    ones0 = jnp.broadcast_to(h0.astype(F32), (DIL_KBLK, pair))
    ones1 = 1.0 - ones0

    for _, dil in C_BRANCHES:
        cls_len = seq // dil
        blocks = sup // (DIL_QBLK * dil)

        def body(tt, carry, dil=dil, cls_len=cls_len, blocks=blocks):
            blk = []
            for u in range(DIL_UNROLL):
                t = tt * DIL_UNROLL + u
                r = lax.div(t, blocks)
                jb = lax.rem(t, blocks)
                c0 = sb * (sup // dil) + DIL_QBLK * jb
                ks = jnp.clip(c0 - C_HALF, 0, cls_len - DIL_KBLK)
                qrow = r + dil * DIL_QBLK * jb
                ssl = pl.ds(qrow, DIL_QBLK, stride=dil)
                ksl = pl.ds(r + dil * ks, DIL_KBLK, stride=dil)
                kf = k_ref[0, ksl, :]
                vf = v_ref[0, ksl, :]
                kcat = jnp.concatenate([jnp.where(h0, kf, 0.0), jnp.where(h0, 0.0, kf)], axis=0)
                vaug = jnp.concatenate(
                    [jnp.concatenate([jnp.where(h0, vf, 0.0), ones0], axis=1),
                     jnp.concatenate([jnp.where(h0, 0.0, vf), ones1], axis=1)], axis=0)
                blk.append(dict(
                    ssl=ssl,
                    q=q_ref[0, pl.ds(p0 + qrow, DIL_QBLK, stride=dil), :].astype(BF16),
                    kcat=kcat.astype(BF16),
                    vaug=vaug.astype(BF16),
                    band=jnp.abs((c0 + qi) - (ks + ki)) <= C_HALF,
                    acc=acc_ref[ssl, :], m=m_ref[ssl, :], l=l_ref[ssl, :]))
            for b in blk:
                b["s"] = lax.dot_general(b["q"], b["kcat"], (((1,), (1,)), ((), ())),
                                         preferred_element_type=F32)
            for b in blk:
                s0 = jnp.where(b["band"], b["s"][:, :DIL_KBLK], NEG_BIG)
                s1 = jnp.where(b["band"], b["s"][:, DIL_KBLK:], NEG_BIG)
                t0 = jnp.maximum(s0[:, :pair], s0[:, pair:])
                t1 = jnp.maximum(s1[:, :pair], s1[:, pair:])
                m0 = jnp.max(jnp.where(h0, jnp.maximum(t0, b["m"]), t0), axis=1, keepdims=True)
                m1 = jnp.max(jnp.where(h0, t1, jnp.maximum(t1, b["m"])), axis=1, keepdims=True)
                m_new = jnp.where(h0, m0, m1)
                b["p"] = jnp.concatenate([jnp.exp2(s0 - m0), jnp.exp2(s1 - m1)],
                                         axis=1).astype(BF16)
                b["alpha"] = jnp.exp2(b["m"] - m_new)
                b["m"] = m_new
            for b in blk:
                pv = jnp.dot(b["p"], b["vaug"], preferred_element_type=F32)
                b["acc"] = b["alpha"] * b["acc"] + pv[:, :pair]
                b["l"] = b["alpha"] * b["l"] + pv[:, pair:]
            for b in blk:
                acc_ref[b["ssl"], :] = b["acc"]
                m_ref[b["ssl"], :] = b["m"]
                l_ref[b["ssl"], :] = b["l"]
            return carry

        lax.fori_loop(0, dil * blocks // DIL_UNROLL, body, 0)

    o_ref[0] = acc_ref[...] / l_ref[...]


def _dilated_attention(qc, kc, vc):
    bsz, seq, width = qc.shape
    sup = min(DIL_SUPER, seq)
    pair = 2 * C_DIM
    in_spec = pl.BlockSpec((1, seq, pair), lambda bi, hp, sb: (bi, 0, hp))
    return pl.pallas_call(
        functools.partial(_dilated_kernel, seq=seq, sup=sup),
        grid=(bsz, width // pair, seq // sup),
        in_specs=[in_spec, in_spec, in_spec],
        out_specs=pl.BlockSpec((1, sup, pair), lambda bi, hp, sb: (bi, sb, hp)),
        out_shape=jax.ShapeDtypeStruct((bsz, seq, width), F32),
        scratch_shapes=[pltpu.VMEM((sup, pair), F32)] * 3,
        compiler_params=_cparams(("parallel", "parallel", "arbitrary")),
        name="dilated",
    )(qc, kc, vc)


def _outproj_kernel(x_ref, oa_ref, od_ref, oc_ref, woa_ref, wob_ref, woc_ref, lam_ref, dg_ref,
                    g_ref, b_ref, o_ref, y_ref, *, lambda_init):
    @pl.when(pl.program_id(0) == 0)
    def _():
        y_ref[...] = jnp.zeros(y_ref.shape, F32)

    ts = x_ref.shape[-1]
    lf = lam_ref[...]
    lam = (jnp.exp(jnp.sum(lf[0:1] * lf[1:2], axis=1, keepdims=True))
           - jnp.exp(jnp.sum(lf[2:3] * lf[3:4], axis=1, keepdims=True)) + lambda_init)
    def tile(ref):
        return jnp.concatenate([ref[0, :, c] for c in range(ref.shape[2])], axis=-1)

    od = tile(od_ref).reshape(B_HEADS, 2, B_V, ts)
    ob = od[:, 0] - lam * od[:, 1]
    ms = jnp.mean(ob * ob, axis=1, keepdims=True)
    ob = ob * lax.rsqrt(ms + NORM_EPS) * dg_ref[...] * (1.0 - lambda_init)
    ob = ob.reshape(B_HEADS * B_V, ts).astype(BF16)
    oa = tile(oa_ref).reshape(A_HEADS * A_V, ts).astype(BF16)
    y = jnp.dot(woa_ref[...], oa, preferred_element_type=F32)
    y += jnp.dot(wob_ref[...], ob, preferred_element_type=F32)
    y += lax.dot_general(woc_ref[...], oc_ref[0].astype(BF16), (((1,), (1,)), ((), ())),
                         preferred_element_type=F32)
    o_ref[0] = _layer_norm_rows(y_ref[...], g_ref[...], b_ref[...])
    y_ref[...] = ALPHA * x_ref[0] + y


def _outproj_layer(xT, oaT, odT, oc, woaT, wobT, wocT, lam, dg, g, b, lambda_init):
    bsz, _, seq = xT.shape
    ts = SEQ_TILE
    nt = seq // ts
    tiles = bsz * nt
    full = lambda a: pl.BlockSpec(a.shape, lambda t: (0,) * a.ndim)
    heads, _, dv, tc = oaT.shape[1:]

    def cur(t):
        tt = jnp.minimum(t, tiles - 1)
        return tt // nt, tt % nt

    def col(t):
        bi, i = cur(t)
        return (bi, 0, i)

    def chunk(t):
        bi, i = cur(t)
        return (bi, 0, i, 0, 0)

    def tokens(t):
        bi, i = cur(t)
        return (bi, i, 0)

    def prev(t):
        tt = jnp.maximum(t - 1, 0)
        return (tt // nt, 0, tt % nt)

    chunks = pl.BlockSpec((1, heads, ts // tc, dv, tc), chunk)
    return pl.pallas_call(
        functools.partial(_outproj_kernel, lambda_init=lambda_init),
        grid=(tiles + 1,),
        in_specs=[
            pl.BlockSpec((1, D_MODEL, ts), col), chunks, chunks,
            pl.BlockSpec((1, ts, C_HEADS * C_DIM), tokens),
            full(woaT), full(wobT), full(wocT), full(lam), full(dg), full(g), full(b),
        ],
        out_specs=pl.BlockSpec((1, D_MODEL, ts), prev),
        out_shape=jax.ShapeDtypeStruct(xT.shape, F32),
        scratch_shapes=[pltpu.VMEM((D_MODEL, ts), F32)],
        compiler_params=_cparams(("arbitrary",)),
        name="outproj",
    )(xT, oaT, odT, oc, woaT, wobT, wocT, lam, dg, g, b)


def _rope_tables(seq, dim):
    inv = 1.0 / (ROPE_THETA ** (jnp.arange(0, dim, 2, dtype=F32) / dim))
    ang = inv[:, None] * jnp.arange(seq, dtype=F32)[None, :]
    return jnp.cos(ang), jnp.sin(ang)


def _prep_layer(i, ln_g, ln_b, ffn_w_gate, ffn_w_up, ffn_w_down, w_in, mla_q_norm, mla_kv_norm,
                mla_w_uq, mla_w_ukv, diff_lambda, diff_subln, w_out):
    tf = D_FF // FF_CHUNKS
    col = lambda v: v.astype(F32)[:, None]
    ffn = []
    for s in range(2):
        wg = ffn_w_gate[i, s].T.reshape(FF_CHUNKS, tf, D_MODEL)
        wu = ffn_w_up[i, s].T.reshape(FF_CHUNKS, tf, D_MODEL)
        wgu = jnp.concatenate([wg, wu], axis=1).astype(BF16)
        wd = ffn_w_down[i, s].T.astype(BF16)
        ffn.append((wgu, wd))
    woT = w_out[i].T.astype(BF16)
    na, nb = A_HEADS * A_V, B_HEADS * B_V
    return dict(
        ffn=ffn,
        ln=[(col(ln_g[i, s]), col(ln_b[i, s])) for s in range(3)],
        winT=w_in[i].T.astype(BF16),
        gq=col(mla_q_norm[i]), gkv=col(mla_kv_norm[i]),
        wuqT=mla_w_uq[i].T.astype(BF16), wukvT=mla_w_ukv[i].T.astype(BF16),
        woaT=woT[:, :na], wobT=woT[:, na:na + nb], wocT=woT[:, na + nb:],
        lam=diff_lambda[i].astype(F32), dg=col(diff_subln[i]),
    )


def _run(x, layers):
    bsz, seq, _ = x.shape
    assert seq % DIL_SUPER == 0 and seq // C_BRANCHES[-1][1] >= DIL_KBLK, seq
    assert seq % SEQ_TILE == 0 and seq % ATTN_TQ == 0 and FF_CHUNKS == 2
    rope = _rope_tables(seq, A_ROPE) + _rope_tables(seq, C_DIM)
    xT = x
    for i, w in enumerate(layers):
        lambda_init = 0.8 - 0.6 * math.exp(-0.3 * i)
        xT = _ffn_layer(xT, *w["ffn"][0], *w["ln"][0], x_tokens=(i == 0))
        qaT, ka, vaT, qdT, kd, vdT, qc, kc, vc = _inproj_layer(
            xT, w["winT"], w["gq"], w["gkv"], w["wuqT"], w["wukvT"], rope)
        oaT = _attention(qaT, ka, vaT, heads=A_HEADS, shared_qk=False, v_heads=A_HEADS)
        odT = _attention(qdT, kd, vdT, heads=2 * B_HEADS, shared_qk=True, v_heads=B_HEADS)
        oc = _dilated_attention(qc, kc, vc)
        xT = _outproj_layer(xT, oaT, odT, oc, w["woaT"], w["wobT"], w["wocT"], w["lam"], w["dg"],
                            *w["ln"][1], lambda_init)
        xT = _ffn_layer(xT, *w["ffn"][1], *w["ln"][2], o_tokens=(i == len(layers) - 1))
    return xT


def kernel(x_prompt, x_sample, ln_g, ln_b, ffn_w_gate, ffn_w_up, ffn_w_down, w_in, mla_q_norm,
           mla_kv_norm, mla_w_uq, mla_w_ukv, diff_lambda, diff_subln, w_out):
    params = (ln_g, ln_b, ffn_w_gate, ffn_w_up, ffn_w_down, w_in, mla_q_norm, mla_kv_norm,
              mla_w_uq, mla_w_ukv, diff_lambda, diff_subln, w_out)
    layers = [_prep_layer(i, *params) for i in range(DEPTH)]
    return _run(x_prompt, layers), _run(x_sample, layers)
```
